```python
import math
import jax, jax.numpy as jnp
from jax import lax
import numpy as np

D_MODEL = 1024
BATCH = 8
SEQ = 4096
DEPTH = 2

CHUNK = 64
N_META = 16
N_SB_HEADS = 8
SB_HEAD_DIM = 64
SB_WIDTH = N_SB_HEADS * SB_HEAD_DIM
CONV_CHANNELS = 512
CONV_KERNEL = 31
MIX_WIDTH = SB_WIDTH + CONV_CHANNELS
IN_COLS = 3 * SB_WIDTH + 2 * CONV_CHANNELS
D_FF = ((8 * D_MODEL // 3 + 255) // 256) * 256
QUERY_BLOCK = 128
EPS = 1e-6

kernel_name = "hybrid_stickbreak_conformer_conv_block"


def _rmsnorm(x, g):
    xf = x.astype(jnp.float32)
    y = xf * lax.rsqrt(jnp.mean(xf * xf, axis=-1, keepdims=True) + EPS)
    return (y * g.astype(jnp.float32)).astype(x.dtype)


def _layernorm(x, g, b):
    xf = x.astype(jnp.float32)
    mu = jnp.mean(xf, axis=-1, keepdims=True)
    var = jnp.mean(jnp.square(xf - mu), axis=-1, keepdims=True)
    y = (xf - mu) * lax.rsqrt(var + EPS)
    return (y * g.astype(jnp.float32) + b.astype(jnp.float32)).astype(x.dtype)


def _stick_breaking_attention(q, k, v):
    b, l, h, dh = q.shape
    n_blocks = -(-l // QUERY_BLOCK)
    lp = n_blocks * QUERY_BLOCK
    pad = [(0, 0), (0, lp - l), (0, 0), (0, 0)]
    q = jnp.pad(q, pad)
    k = jnp.pad(k, pad)
    v = jnp.pad(v, pad)
    scale = 1.0 / math.sqrt(dh)
    q_blocks = q.reshape(b, n_blocks, QUERY_BLOCK, h, dh).transpose(1, 0, 2, 3, 4)
    starts = jnp.arange(n_blocks, dtype=jnp.int32) * QUERY_BLOCK
    key_pos = jnp.arange(lp, dtype=jnp.int32)

    def one_block(args):
        qi, start = args
        z = jnp.einsum('bqhd,bkhd->bhqk', qi, k).astype(jnp.float32) * scale
        t = start + jnp.arange(QUERY_BLOCK, dtype=jnp.int32)
        mask = key_pos[None, :] < t[:, None]
        log_beta = jax.nn.log_sigmoid(z)
        log_1m_beta = jnp.where(mask, jax.nn.log_sigmoid(-z), 0.0)
        rev = lax.cumsum(log_1m_beta, axis=3, reverse=True)
        excl = jnp.concatenate([rev[..., 1:], jnp.zeros_like(rev[..., :1])], axis=-1)
        w = jnp.where(mask, jnp.exp(log_beta + excl), 0.0)
        return jnp.einsum('bhqk,bkhd->bqhd', w.astype(v.dtype), v)

    out = lax.map(one_block, (q_blocks, starts))
    out = out.transpose(1, 0, 2, 3, 4).reshape(b, lp, h, dh)
    return out[:, :l]


def _conformer_conv(a, gate, dw_w, dw_b, ln_g, ln_b):
    u = a * jax.nn.sigmoid(gate)
    c = u.shape[-1]
    u = lax.conv_general_dilated(
        u, dw_w.astype(u.dtype)[:, None, :], window_strides=(1,),
        padding=[(CONV_KERNEL - 1, 0)],
        dimension_numbers=('NWC', 'WIO', 'NWC'), feature_group_count=c)
    u = u + dw_b.astype(u.dtype)
    u = _layernorm(u, ln_g, ln_b)
    return jax.nn.silu(u)


def setup_inputs(seed: int = 0) -> dict:
    key = jax.random.key(seed)
    ks = jax.random.split(key, 16)
    f32 = jnp.float32
    nrm = lambda k, shape, s: jax.random.normal(k, shape, f32) * s
    return {
        "x": jax.random.normal(ks[0], (BATCH, SEQ, D_MODEL), f32),
        "meta_tokens": nrm(ks[1], (N_META, D_MODEL), 1.0),
        "mix_norm_g": 1.0 + nrm(ks[2], (DEPTH, D_MODEL), 0.02),
        "w_in": nrm(ks[3], (DEPTH, D_MODEL, IN_COLS), D_MODEL ** -0.5),
        "conv_dw_w": nrm(ks[4], (DEPTH, CONV_KERNEL, CONV_CHANNELS), CONV_KERNEL ** -0.5),
        "conv_dw_b": nrm(ks[5], (DEPTH, CONV_CHANNELS), 0.02),
        "conv_ln_g": 1.0 + nrm(ks[6], (DEPTH, CONV_CHANNELS), 0.02),
        "conv_ln_b": nrm(ks[7], (DEPTH, CONV_CHANNELS), 0.02),
        "w_out": nrm(ks[8], (DEPTH, MIX_WIDTH, D_MODEL), MIX_WIDTH ** -0.5),
        "ffn_norm_g": 1.0 + nrm(ks[9], (DEPTH, D_MODEL), 0.02),
        "w_gate": nrm(ks[10], (DEPTH, D_MODEL, D_FF), D_MODEL ** -0.5),
        "w_up": nrm(ks[11], (DEPTH, D_MODEL, D_FF), D_MODEL ** -0.5),
        "w_down": nrm(ks[12], (DEPTH, D_FF, D_MODEL), D_FF ** -0.5),
        "final_norm_g": 1.0 + nrm(ks[13], (D_MODEL,), 0.02),
    }


def reference(x, meta_tokens, mix_norm_g, w_in, conv_dw_w, conv_dw_b, conv_ln_g, conv_ln_b,
              w_out, ffn_norm_g, w_gate, w_up, w_down, final_norm_g):
    b = x.shape[0]
    meta = jnp.broadcast_to(meta_tokens.astype(x.dtype)[None], (b, N_META, D_MODEL))
    h = jnp.concatenate([meta, x], axis=1)
    l = h.shape[1]
    for i in range(DEPTH):
        hn = _rmsnorm(h, mix_norm_g[i])
        proj = jnp.einsum('bld,dc->blc', hn, w_in[i])
        q, k, v, ca, cg = jnp.split(
            proj, [SB_WIDTH, 2 * SB_WIDTH, 3 * SB_WIDTH, 3 * SB_WIDTH + CONV_CHANNELS], axis=-1)
        heads = lambda t: t.reshape(b, l, N_SB_HEADS, SB_HEAD_DIM)
        attn = _stick_breaking_attention(heads(q), heads(k), heads(v)).reshape(b, l, SB_WIDTH)
        conv = _conformer_conv(ca, cg, conv_dw_w[i], conv_dw_b[i], conv_ln_g[i], conv_ln_b[i])
        mixed = jnp.concatenate([attn, conv], axis=-1)
        h = h + jnp.einsum('blc,cd->bld', mixed, w_out[i])
        hn = _rmsnorm(h, ffn_norm_g[i])
        g = jnp.einsum('bld,df->blf', hn, w_gate[i])
        u = jnp.einsum('bld,df->blf', hn, w_up[i])
        h = h + jnp.einsum('blf,fd->bld', jax.nn.silu(g) * u, w_down[i])
    h = _rmsnorm(h, final_norm_g)
    return h[:, N_META:]
```

```python
import functools
import math

import jax
import jax.numpy as jnp
from jax import lax
from jax.experimental import pallas as pl
from jax.experimental.pallas import tpu as pltpu

F32 = jnp.float32
BF16 = jnp.bfloat16

EPS = 1e-6
N_HEADS = 8
HEAD_DIM = 64
SB_WIDTH = N_HEADS * HEAD_DIM
LANES = 128
HEADS_PER_BLOCK = LANES // HEAD_DIM
EXP_ZERO_CUT = -104.0
BAND_BLOCKS = 3
VMEM_LIMIT = 56 * 1024 * 1024


def _row_tile(rows, target):
    best = 8
    for t in range(8, min(rows, target) + 1, 8):
        if rows % t == 0:
            best = t
    return best


def _const_spec(shape):
    zeros = (0,) * len(shape)
    return pl.BlockSpec(shape, lambda *_: zeros, pipeline_mode=pl.Buffered(1))


def _rmsnorm(x, g):
    ms = jnp.mean(x * x, axis=-1, keepdims=True)
    return x * lax.rsqrt(ms + EPS) * g


def _in_proj_kernel(h_ref, g_ref, w_ref, q_ref, k_ref, v_ref, a_ref, gate_ref, *, q_scale):
    hn = _rmsnorm(h_ref[...], g_ref[...]).astype(BF16)
    outs = (q_ref, k_ref, v_ref, a_ref, gate_ref)
    col = 0
    for idx, o_ref in enumerate(outs):
        width = o_ref.shape[-1]
        y = jnp.dot(hn, w_ref[:, col:col + width], preferred_element_type=F32)
        if idx == 0:
            y = y * q_scale
        o_ref[...] = y.astype(o_ref.dtype)
        col += width


def _in_proj(h2d, g, w_bf16, conv_ch):
    rows, d = h2d.shape
    tm = _row_tile(rows, 1024)
    widths = (SB_WIDTH, SB_WIDTH, SB_WIDTH, conv_ch, conv_ch)
    dtypes = (BF16, BF16, BF16, F32, F32)
    return pl.pallas_call(
        functools.partial(_in_proj_kernel, q_scale=1.0 / math.sqrt(HEAD_DIM)),
        out_shape=[jax.ShapeDtypeStruct((rows, w), dt) for w, dt in zip(widths, dtypes)],
        grid=(rows // tm,),
        in_specs=[
            pl.BlockSpec((tm, d), lambda i: (i, 0)),
            _const_spec((1, d)),
            _const_spec(w_bf16.shape),
        ],
        out_specs=[pl.BlockSpec((tm, w), lambda i: (i, 0)) for w in widths],
        compiler_params=pltpu.CompilerParams(
            dimension_semantics=("parallel",), vmem_limit_bytes=VMEM_LIMIT),
        name="in_proj",
    )(h2d, g.reshape(1, d), w_bf16)


def _sb_tile(qh, k_blk, v_blk, cs_w, carry, mask):
    z = lax.dot_general(qh, k_blk, (((1,), (1,)), ((), ())), preferred_element_type=F32)
    softplus = jnp.maximum(z, 0.0) + jnp.log(1.0 + jnp.exp(-jnp.abs(z)))
    log_1m_beta = -softplus
    if mask is not None:
        log_1m_beta = jnp.where(mask, log_1m_beta, 0.0)
    log_beta = z - softplus
    hi = log_1m_beta.astype(BF16)
    lo = (log_1m_beta - hi.astype(F32)).astype(BF16)
    sums = jnp.dot(jnp.concatenate([hi, lo], axis=1), cs_w, preferred_element_type=F32)
    w = jnp.exp(log_beta + sums[:, :LANES] + carry)
    if mask is not None:
        w = jnp.where(mask, w, 0.0)
    pv = jnp.dot(w.astype(BF16), v_blk, preferred_element_type=F32)
    return pv, carry + sums[:, LANES:]


def _attn_kernel(q_ref, k_ref, v_ref, csw_ref, o_ref, acc_ref, carry_ref):
    n_blocks = q_ref.shape[1] // LANES
    cs_w = csw_ref[...]
    lane = lax.broadcasted_iota(jnp.int32, (LANES, LANES), 1)
    row = lax.broadcasted_iota(jnp.int32, (LANES, LANES), 0)
    diag_mask = lane < row
    head_lanes = [(lane // HEAD_DIM) == h for h in range(HEADS_PER_BLOCK)]

    def rows(i):
        if isinstance(i, int):
            return pl.ds(i * LANES, LANES)
        return pl.ds(pl.multiple_of(i * LANES, LANES), LANES)

    def band_block(i, n_keys_blocks):
        q2 = q_ref[0, rows(i), :]
        out = jnp.zeros((LANES, LANES), F32)
        worst = None
        for h in range(HEADS_PER_BLOCK):
            qh = jnp.where(head_lanes[h], q2, jnp.zeros_like(q2))
            carry = jnp.zeros((LANES, LANES), F32)
            acc = jnp.zeros((LANES, LANES), F32)
            for d in range(n_keys_blocks):
                j = i - d
                pv, carry = _sb_tile(qh, k_ref[0, rows(j), :], v_ref[0, rows(j), :], cs_w,
                                     carry, diag_mask if d == 0 else None)
                acc = acc + pv
            out = jnp.where(head_lanes[h], acc, out)
            worst = carry if worst is None else jnp.maximum(worst, carry)
        o_ref[0, rows(i), :] = out.astype(o_ref.dtype)
        return worst

    n_peel = min(BAND_BLOCKS, n_blocks)
    for i in range(n_peel):
        band_block(i, i + 1)

    def body(i, worst):
        return jnp.maximum(worst, band_block(i, BAND_BLOCKS))

    worst = lax.fori_loop(n_peel, n_blocks, body,
                          jnp.full((LANES, LANES), -jnp.inf, F32))

    @pl.when(jnp.max(worst) > EXP_ZERO_CUT)
    def _():
        def q_block(i, _):
            q2 = q_ref[0, rows(i), :]
            q_pos = i * LANES + row
            out = jnp.zeros((LANES, LANES), F32)
            for h in range(HEADS_PER_BLOCK):
                qh = jnp.where(head_lanes[h], q2, jnp.zeros_like(q2))
                acc_ref[...] = jnp.zeros_like(acc_ref)
                carry_ref[...] = jnp.zeros_like(carry_ref)

                def cond(state):
                    j, alive = state
                    return jnp.logical_and(j >= 0, alive > 0)

                def step(state):
                    j, _ = state
                    mask = (j * LANES + lane) < q_pos
                    pv, carry = _sb_tile(qh, k_ref[0, rows(j), :], v_ref[0, rows(j), :],
                                         cs_w, carry_ref[...], mask)
                    acc_ref[...] += pv
                    carry_ref[...] = carry
                    alive = (jnp.max(carry) > EXP_ZERO_CUT).astype(jnp.int32)
                    return j - 1, alive

                lax.while_loop(cond, step, (i, jnp.int32(1)))
                out = jnp.where(head_lanes[h], acc_ref[...], out)
            o_ref[0, rows(i), :] = out.astype(o_ref.dtype)
            return 0

        lax.fori_loop(0, n_blocks, q_block, 0)


def _cumsum_weights():
    j = jnp.arange(LANES)[:, None]
    s = jnp.arange(LANES)[None, :]
    half = jnp.concatenate([(j > s).astype(BF16), jnp.ones((LANES, LANES), BF16)], axis=1)
    return jnp.concatenate([half, half], axis=0)


def _attention(q, k, v):
    b, lp, width = q.shape
    spec = pl.BlockSpec((1, lp, LANES), lambda bi, hi: (bi, 0, hi))
    return pl.pallas_call(
        _attn_kernel,
        out_shape=jax.ShapeDtypeStruct((b, lp, width), BF16),
        grid=(b, width // LANES),
        in_specs=[spec, spec, spec, _const_spec((2 * LANES, 2 * LANES))],
        out_specs=spec,
        scratch_shapes=[pltpu.VMEM((LANES, LANES), F32), pltpu.VMEM((LANES, LANES), F32)],
        compiler_params=pltpu.CompilerParams(
            dimension_semantics=("parallel", "parallel"), vmem_limit_bytes=VMEM_LIMIT),
        name="sb_attention",
    )(q, k, v, _cumsum_weights())


CONV_HALO = 32
CONV_CHUNK = 64
CONV_TILE_ROWS = 1056


def _conv_kernel(a_ref, gate_ref, w_ref, b_ref, lng_ref, lnb_ref, o_ref, u_ref, *, n_taps):
    t = pl.program_id(1)
    tl = a_ref.shape[1]

    @pl.when(t == 0)
    def _():
        u_ref[0:CONV_HALO, :] = jnp.zeros((CONV_HALO, u_ref.shape[1]), F32)

    @pl.when(t > 0)
    def _():
        u_ref[0:CONV_HALO, :] = u_ref[tl:tl + CONV_HALO, :]

    u_ref[CONV_HALO:CONV_HALO + tl, :] = a_ref[0] * jax.nn.sigmoid(gate_ref[0])

    w = w_ref[...]
    bias = b_ref[...]
    ln_g = lng_ref[...]
    ln_b = lnb_ref[...]

    def chunk(c, _):
        base = pl.multiple_of(c * CONV_CHUNK, 8)
        window = u_ref[pl.ds(base, CONV_CHUNK + CONV_HALO), :]
        acc = jnp.broadcast_to(bias, (CONV_CHUNK, bias.shape[1]))
        for r in range(8):
            shifted = window if r == 0 else pltpu.roll(window, r, 0)
            for a8 in range(0, n_taps, 8):
                s = a8 + r
                if s >= n_taps:
                    continue
                tap = n_taps - 1 - s
                lo = CONV_HALO - a8
                acc = acc + w[tap:tap + 1, :] * shifted[lo:lo + CONV_CHUNK, :]
        mu = jnp.mean(acc, axis=-1, keepdims=True)
        cen = acc - mu
        var = jnp.mean(cen * cen, axis=-1, keepdims=True)
        y = cen * lax.rsqrt(var + EPS) * ln_g + ln_b
        o_ref[0, pl.ds(base, CONV_CHUNK), :] = (y * jax.nn.sigmoid(y)).astype(o_ref.dtype)
        return 0

    lax.fori_loop(0, tl // CONV_CHUNK, chunk, 0)


def _conformer_conv(a, gate, dw_w, dw_b, ln_g, ln_b):
    b, lp, c = a.shape
    n_taps = dw_w.shape[0]
    assert n_taps - 1 <= CONV_HALO and lp % CONV_CHUNK == 0
    tl = CONV_CHUNK * max(n for n in range(1, lp // CONV_CHUNK + 1)
                          if (lp // CONV_CHUNK) % n == 0 and n * CONV_CHUNK <= CONV_TILE_ROWS)
    tile = pl.BlockSpec((1, tl, c), lambda bi, ti: (bi, ti, 0))
    vec = _const_spec((1, c))
    return pl.pallas_call(
        functools.partial(_conv_kernel, n_taps=n_taps),
        out_shape=jax.ShapeDtypeStruct((b, lp, c), BF16),
        grid=(b, lp // tl),
        in_specs=[tile, tile, _const_spec((n_taps, c)), vec, vec, vec],
        out_specs=tile,
        scratch_shapes=[pltpu.VMEM((CONV_HALO + tl, c), F32)],
        compiler_params=pltpu.CompilerParams(
            dimension_semantics=("parallel", "arbitrary"), vmem_limit_bytes=VMEM_LIMIT),
        name="conformer_conv",
    )(a, gate, dw_w, dw_b.reshape(1, c), ln_g.reshape(1, c), ln_b.reshape(1, c))


def _mix_ffn_kernel(h_ref, attn_ref, conv_ref, wo_ref, g_ref, wg_ref, wu_ref, wd_ref, fg_ref,
                    o_ref, *, final_norm):
    sb = attn_ref.shape[-1]
    h = h_ref[...]
    h = h + jnp.dot(attn_ref[...], wo_ref[0:sb, :], preferred_element_type=F32)
    h = h + jnp.dot(conv_ref[...], wo_ref[sb:, :], preferred_element_type=F32)
    hn = _rmsnorm(h, g_ref[...]).astype(BF16)
    gate = jnp.dot(hn, wg_ref[...], preferred_element_type=F32)
    up = jnp.dot(hn, wu_ref[...], preferred_element_type=F32)
    act = (gate * jax.nn.sigmoid(gate) * up).astype(BF16)
    h = h + jnp.dot(act, wd_ref[...], preferred_element_type=F32)
    if final_norm:
        h = _rmsnorm(h, fg_ref[...])
    o_ref[...] = h


def _mix_ffn(h2d, attn2d, conv2d, wo, g, wg, wu, wd, final_g, final_norm):
    rows, d = h2d.shape
    tm = _row_tile(rows, 512)
    row_spec = lambda w: pl.BlockSpec((tm, w), lambda i: (i, 0))
    return pl.pallas_call(
        functools.partial(_mix_ffn_kernel, final_norm=final_norm),
        out_shape=jax.ShapeDtypeStruct((rows, d), F32),
        grid=(rows // tm,),
        in_specs=[
            row_spec(d), row_spec(attn2d.shape[1]), row_spec(conv2d.shape[1]),
            _const_spec(wo.shape), _const_spec((1, d)),
            _const_spec(wg.shape), _const_spec(wu.shape), _const_spec(wd.shape),
            _const_spec((1, d)),
        ],
        out_specs=row_spec(d),
        compiler_params=pltpu.CompilerParams(
            dimension_semantics=("parallel",), vmem_limit_bytes=VMEM_LIMIT),
        name="mix_ffn",
    )(h2d, attn2d, conv2d, wo, g.reshape(1, d), wg, wu, wd, final_g.reshape(1, d))


def kernel(x, meta_tokens, mix_norm_g, w_in, conv_dw_w, conv_dw_b, conv_ln_g, conv_ln_b,
           w_out, ffn_norm_g, w_gate, w_up, w_down, final_norm_g):
    b, seq, d = x.shape
    n_meta = meta_tokens.shape[0]
    depth = w_in.shape[0]
    conv_ch = conv_dw_w.shape[-1]
    l = n_meta + seq
    lp = -(-l // LANES) * LANES

    meta = jnp.broadcast_to(meta_tokens.astype(x.dtype)[None], (b, n_meta, d))
    h = jnp.concatenate([meta, x, jnp.zeros((b, lp - l, d), x.dtype)], axis=1)
    h = h.reshape(b * lp, d)

    for i in range(depth):
        q, k, v, ca, cg = _in_proj(h, mix_norm_g[i], w_in[i].astype(BF16), conv_ch)
        seq3 = lambda t: t.reshape(b, lp, t.shape[-1])
        attn = _attention(seq3(q), seq3(k), seq3(v))
        conv = _conformer_conv(seq3(ca), seq3(cg), conv_dw_w[i], conv_dw_b[i],
                               conv_ln_g[i], conv_ln_b[i])
        h = _mix_ffn(h, attn.reshape(b * lp, -1), conv.reshape(b * lp, -1),
                     w_out[i].astype(BF16), ffn_norm_g[i], w_gate[i].astype(BF16),
                     w_up[i].astype(BF16), w_down[i].astype(BF16), final_norm_g,
                     final_norm=(i == depth - 1))
    return h.reshape(b, lp, d)[:, n_meta:l]
```

```python
import functools
import math

import jax
import jax.numpy as jnp
from jax import lax
from jax.experimental import pallas as pl
from jax.experimental.pallas import tpu as pltpu

F32 = jnp.float32
BF16 = jnp.bfloat16

EPS = 1e-6
N_HEADS = 8
HEAD_DIM = 64
SB_WIDTH = N_HEADS * HEAD_DIM
LANES = 128
HEADS_PER_BLOCK = LANES // HEAD_DIM
EXP_ZERO_CUT = -104.0
LOG_ZERO = -1e30
BAND = 3
TILES = BAND * HEADS_PER_BLOCK
VMEM_LIMIT = 56 * 1024 * 1024


def _row_tile(rows, target):
    best = 8
    for t in range(8, min(rows, target) + 1, 8):
        if rows % t == 0:
            best = t
    return best


def _const_spec(shape):
    zeros = (0,) * len(shape)
    return pl.BlockSpec(shape, lambda *_: zeros, pipeline_mode=pl.Buffered(1))


def _rmsnorm(x, g):
    ms = jnp.mean(x * x, axis=-1, keepdims=True)
    return x * lax.rsqrt(ms + EPS) * g


def _in_proj_kernel(h_ref, g_ref, w_ref, wkt_ref, q_ref, kt_ref, v_ref, a_ref, gate_ref, *,
                    q_scale):
    hn = _rmsnorm(h_ref[...], g_ref[...]).astype(BF16)
    col = 0
    for idx, o_ref in enumerate((q_ref, v_ref, a_ref, gate_ref)):
        width = o_ref.shape[-1]
        y = jnp.dot(hn, w_ref[:, col:col + width], preferred_element_type=F32)
        if idx == 0:
            y = y * q_scale
        o_ref[...] = y.astype(o_ref.dtype)
        col += width
    kt = lax.dot_general(wkt_ref[...], hn, (((1,), (1,)), ((), ())), preferred_element_type=F32)
    kt_ref[...] = kt.astype(kt_ref.dtype)


def _in_proj(h2d, g, w_qvag, w_kt, conv_ch):
    rows, d = h2d.shape
    tm = _row_tile(rows, 1024)
    widths = (SB_WIDTH, SB_WIDTH, conv_ch, conv_ch)
    dtypes = (BF16, BF16, F32, F32)
    row_out = lambda w, dt: (jax.ShapeDtypeStruct((rows, w), dt),
                             pl.BlockSpec((tm, w), lambda i: (i, 0)))
    outs = [row_out(widths[0], dtypes[0]),
            (jax.ShapeDtypeStruct((SB_WIDTH, rows), BF16),
             pl.BlockSpec((SB_WIDTH, tm), lambda i: (0, i)))]
    outs += [row_out(w, dt) for w, dt in zip(widths[1:], dtypes[1:])]
    return pl.pallas_call(
        functools.partial(_in_proj_kernel, q_scale=1.0 / math.sqrt(HEAD_DIM)),
        out_shape=[o[0] for o in outs],
        grid=(rows // tm,),
        in_specs=[
            pl.BlockSpec((tm, d), lambda i: (i, 0)),
            _const_spec((1, d)),
            _const_spec(w_qvag.shape),
            _const_spec(w_kt.shape),
        ],
        out_specs=[o[1] for o in outs],
        compiler_params=pltpu.CompilerParams(
            dimension_semantics=("parallel",), vmem_limit_bytes=VMEM_LIMIT),
        name="in_proj",
    )(h2d, g.reshape(1, d), w_qvag, w_kt)


def _log_terms(z, mask):
    softplus = jnp.maximum(z, 0.0) + jnp.log(1.0 + jnp.exp(-jnp.abs(z)))
    log_1m_beta = -softplus
    log_beta = z - softplus
    if mask is not None:
        log_1m_beta = jnp.where(mask, log_1m_beta, 0.0)
        log_beta = jnp.where(mask, log_beta, LOG_ZERO)
    return log_1m_beta, log_beta


def _hi_lo(x):
    hi = x.astype(BF16)
    return hi, (x - hi.astype(F32)).astype(BF16)


def _attn_kernel(q_ref, kt_ref, v_ref, csw_ref, o_ref,
                 kst_ref, vst_ref, hilo0, hilo1, lb0, lb1, ws0, ws1, acc_ref, carry_ref):
    n_blocks = q_ref.shape[1] // LANES
    hilo_s, lb_s, ws_s = (hilo0, hilo1), (lb0, lb1), (ws0, ws1)
    lane = lax.broadcasted_iota(jnp.int32, (LANES, LANES), 1)
    row = lax.broadcasted_iota(jnp.int32, (LANES, LANES), 0)
    diag_mask = lane < row
    zero_half = jnp.zeros((HEAD_DIM, LANES), BF16)
    zero_blk = jnp.zeros((LANES, LANES), BF16)

    def rows(i, n=LANES):
        if isinstance(i, int):
            return pl.ds(i * n, n)
        return pl.ds(pl.multiple_of(i * n, n), n)

    for j in range(BAND - 1):
        kst_ref[j] = jnp.zeros(kst_ref.shape[1:], BF16)
        vst_ref[rows(j, 2 * LANES), :] = jnp.zeros((2 * LANES, LANES), BF16)
    for j in range(n_blocks):
        kt = kt_ref[:, j * LANES:(j + 1) * LANES]
        top = jnp.concatenate([kt[:HEAD_DIM], zero_half], axis=1)
        bottom = jnp.concatenate([zero_half, kt[HEAD_DIM:]], axis=1)
        kst_ref[j + BAND - 1] = jnp.concatenate([top, bottom], axis=0)

    def stack_v(j, _):
        v = v_ref[0, rows(j), :]
        base = pl.multiple_of((j + BAND - 1) * 2 * LANES, 2 * LANES)
        vst_ref[pl.ds(base, LANES), :] = jnp.where(lane < HEAD_DIM, v, zero_blk)
        vst_ref[pl.ds(base + LANES, LANES), :] = jnp.where(lane < HEAD_DIM, zero_blk, v)
        return 0

    lax.fori_loop(0, n_blocks, stack_v, 0)

    def stage_logits(i, s, first_valid=0):
        q2 = q_ref[0, rows(i), :]
        for p in range(BAND):
            for h in range(HEADS_PER_BLOCK):
                t = HEADS_PER_BLOCK * p + h
                if p < first_valid:
                    hilo_s[s][rows(t), :] = jnp.zeros((LANES, 2 * LANES), BF16)
                    lb_s[s][rows(t), :] = jnp.full((LANES, LANES), LOG_ZERO, F32)
            if p < first_valid:
                continue
            z2 = jnp.dot(q2, kst_ref[i + p], preferred_element_type=F32)
            for h in range(HEADS_PER_BLOCK):
                t = HEADS_PER_BLOCK * p + h
                log_1m_beta, log_beta = _log_terms(z2[:, h * LANES:(h + 1) * LANES],
                                                   diag_mask if p == BAND - 1 else None)
                hi, lo = _hi_lo(log_1m_beta)
                hilo_s[s][rows(t), 0:LANES] = hi
                hilo_s[s][rows(t), LANES:] = lo
                lb_s[s][rows(t), :] = log_beta

    def stage_weights(s):
        sums = jnp.dot(hilo_s[s][...], csw_ref[...], preferred_element_type=F32)
        worst = None
        for h in range(HEADS_PER_BLOCK):
            carry = jnp.zeros((LANES, LANES), F32)
            for p in reversed(range(BAND)):
                t = HEADS_PER_BLOCK * p + h
                blk = slice(t * LANES, (t + 1) * LANES)
                e = lb_s[s][blk, :] + sums[blk, :LANES] + carry
                ws_s[s][:, blk] = jnp.exp(e).astype(BF16)
                carry = carry + sums[blk, LANES:]
            worst = carry if worst is None else jnp.maximum(worst, carry)
        return worst

    def stage_output(i, s):
        start = i * 2 * LANES
        if not isinstance(i, int):
            start = pl.multiple_of(start, 2 * LANES)
        v_band = vst_ref[pl.ds(start, BAND * 2 * LANES), :]
        out = jnp.dot(ws_s[s][...], v_band, preferred_element_type=F32)
        o_ref[0, rows(i), :] = out.astype(o_ref.dtype)

    def step(i, s, worst):
        stage_logits(i, s)
        w = stage_weights(1 - s)
        stage_output(i - 2, s)
        return jnp.where(i - 1 >= BAND, jnp.maximum(worst, w), worst)

    worst = jnp.full((LANES, LANES), -jnp.inf, F32)
    n_pro = min(BAND - 1, n_blocks)
    for i in range(n_pro):
        stage_logits(i, i % 2, first_valid=BAND - 1 - i)
        if i >= 1:
            stage_weights((i - 1) % 2)
        if i >= 2:
            stage_output(i - 2, i % 2)
    n_main = n_blocks - n_pro
    if n_main > 0:
        assert n_pro % 2 == 0

        def pair(t, worst):
            i = n_pro + 2 * t
            worst = step(i, 0, worst)
            return step(i + 1, 1, worst)

        worst = lax.fori_loop(0, n_main // 2, pair, worst)
        if n_main % 2:
            worst = step(n_blocks - 1, (n_blocks - 1) % 2, worst)
    last = n_blocks - 1
    w = stage_weights(last % 2)
    if last >= BAND:
        worst = jnp.maximum(worst, w)
    if last >= 1:
        stage_output(last - 1, (last - 1) % 2)
    stage_output(last, last % 2)

    @pl.when(jnp.max(worst) > EXP_ZERO_CUT)
    def _():
        def q_block(i, _):
            q2 = q_ref[0, rows(i), :]
            q_pos = i * LANES + row
            acc_ref[...] = jnp.zeros_like(acc_ref)
            carry_ref[...] = jnp.zeros_like(carry_ref)

            def cond(state):
                j, alive = state
                return jnp.logical_and(j >= 0, alive > 0)

            def body(state):
                j, _ = state
                mask = (j * LANES + lane) < q_pos
                z2 = jnp.dot(q2, kst_ref[j + BAND - 1], preferred_element_type=F32)
                ws = []
                alive = None
                for h in range(HEADS_PER_BLOCK):
                    log_1m_beta, log_beta = _log_terms(z2[:, h * LANES:(h + 1) * LANES], mask)
                    sums = jnp.dot(jnp.concatenate(_hi_lo(log_1m_beta), axis=1), csw_ref[...],
                                   preferred_element_type=F32)
                    carry = carry_ref[h]
                    ws.append(jnp.exp(log_beta + sums[:, :LANES] + carry).astype(BF16))
                    carry = carry + sums[:, LANES:]
                    carry_ref[h] = carry
                    alive = carry if alive is None else jnp.maximum(alive, carry)
                base = pl.multiple_of((j + BAND - 1) * 2 * LANES, 2 * LANES)
                acc_ref[...] += jnp.dot(jnp.concatenate(ws, axis=1),
                                        vst_ref[pl.ds(base, 2 * LANES), :],
                                        preferred_element_type=F32)
                return j - 1, (jnp.max(alive) > EXP_ZERO_CUT).astype(jnp.int32)

            lax.while_loop(cond, body, (i, jnp.int32(1)))
            o_ref[0, rows(i), :] = acc_ref[...].astype(o_ref.dtype)
            return 0

        lax.fori_loop(0, n_blocks, q_block, 0)


def _cumsum_weights():
    j = jnp.arange(LANES)[:, None]
    s = jnp.arange(LANES)[None, :]
    half = jnp.concatenate([(j > s).astype(BF16), jnp.ones((LANES, LANES), BF16)], axis=1)
    return jnp.concatenate([half, half], axis=0)


def _attention(q, kt, v):
    b, lp, width = q.shape
    n_blocks = lp // LANES
    spec = pl.BlockSpec((1, lp, LANES), lambda bi, hi: (bi, 0, hi))
    stage = lambda shape, dt: [pltpu.VMEM(shape, dt), pltpu.VMEM(shape, dt)]
    return pl.pallas_call(
        _attn_kernel,
        out_shape=jax.ShapeDtypeStruct((b, lp, width), BF16),
        grid=(b, width // LANES),
        in_specs=[spec, pl.BlockSpec((LANES, lp), lambda bi, hi: (hi, bi)), spec,
                  _const_spec((2 * LANES, 2 * LANES))],
        out_specs=spec,
        scratch_shapes=[
            pltpu.VMEM((n_blocks + BAND - 1, LANES, 2 * LANES), BF16),
            pltpu.VMEM(((n_blocks + BAND - 1) * 2 * LANES, LANES), BF16),
            *stage((TILES * LANES, 2 * LANES), BF16),
            *stage((TILES * LANES, LANES), F32),
            *stage((LANES, TILES * LANES), BF16),
            pltpu.VMEM((LANES, LANES), F32),
            pltpu.VMEM((HEADS_PER_BLOCK, LANES, LANES), F32),
        ],
        compiler_params=pltpu.CompilerParams(
            dimension_semantics=("parallel", "parallel"), vmem_limit_bytes=VMEM_LIMIT),
        name="sb_attention",
    )(q, kt, v, _cumsum_weights())


CONV_HALO = 32
CONV_CHUNK = 64
CONV_TILE_ROWS = 1056


def _conv_kernel(a_ref, gate_ref, w_ref, b_ref, lng_ref, lnb_ref, o_ref, u_ref, *, n_taps):
    t = pl.program_id(1)
    tl = a_ref.shape[1]

    @pl.when(t == 0)
    def _():
        u_ref[0:CONV_HALO, :] = jnp.zeros((CONV_HALO, u_ref.shape[1]), F32)

    @pl.when(t > 0)
    def _():
        u_ref[0:CONV_HALO, :] = u_ref[tl:tl + CONV_HALO, :]

    u_ref[CONV_HALO:CONV_HALO + tl, :] = a_ref[0] * jax.nn.sigmoid(gate_ref[0])

    w = w_ref[...]
    bias = b_ref[...]
    ln_g = lng_ref[...]
    ln_b = lnb_ref[...]

    def chunk(c, _):
        base = pl.multiple_of(c * CONV_CHUNK, 8)
        window = u_ref[pl.ds(base, CONV_CHUNK + CONV_HALO), :]
        acc = jnp.broadcast_to(bias, (CONV_CHUNK, bias.shape[1]))
        for r in range(8):
            shifted = window if r == 0 else pltpu.roll(window, r, 0)
            for a8 in range(0, n_taps, 8):
                s = a8 + r
                if s >= n_taps:
                    continue
                tap = n_taps - 1 - s
                lo = CONV_HALO - a8
                acc = acc + w[tap:tap + 1, :] * shifted[lo:lo + CONV_CHUNK, :]
        mu = jnp.mean(acc, axis=-1, keepdims=True)
        cen = acc - mu
        var = jnp.mean(cen * cen, axis=-1, keepdims=True)
        y = cen * lax.rsqrt(var + EPS) * ln_g + ln_b
        o_ref[0, pl.ds(base, CONV_CHUNK), :] = (y * jax.nn.sigmoid(y)).astype(o_ref.dtype)
        return 0

    lax.fori_loop(0, tl // CONV_CHUNK, chunk, 0)


def _conformer_conv(a, gate, dw_w, dw_b, ln_g, ln_b):
    b, lp, c = a.shape
    n_taps = dw_w.shape[0]
    assert n_taps - 1 <= CONV_HALO and lp % CONV_CHUNK == 0
    tl = CONV_CHUNK * max(n for n in range(1, lp // CONV_CHUNK + 1)
                          if (lp // CONV_CHUNK) % n == 0 and n * CONV_CHUNK <= CONV_TILE_ROWS)
    tile = pl.BlockSpec((1, tl, c), lambda bi, ti: (bi, ti, 0))
    vec = _const_spec((1, c))
    return pl.pallas_call(
        functools.partial(_conv_kernel, n_taps=n_taps),
        out_shape=jax.ShapeDtypeStruct((b, lp, c), BF16),
        grid=(b, lp // tl),
        in_specs=[tile, tile, _const_spec((n_taps, c)), vec, vec, vec],
        out_specs=tile,
        scratch_shapes=[pltpu.VMEM((CONV_HALO + tl, c), F32)],
        compiler_params=pltpu.CompilerParams(
            dimension_semantics=("parallel", "arbitrary"), vmem_limit_bytes=VMEM_LIMIT),
        name="conformer_conv",
    )(a, gate, dw_w, dw_b.reshape(1, c), ln_g.reshape(1, c), ln_b.reshape(1, c))


def _mix_ffn_kernel(h_ref, attn_ref, conv_ref, wo_ref, g_ref, wg_ref, wu_ref, wd_ref, fg_ref,
                    o_ref, *, final_norm):
    sb = attn_ref.shape[-1]
    h = h_ref[...]
    h = h + jnp.dot(attn_ref[...], wo_ref[0:sb, :], preferred_element_type=F32)
    h = h + jnp.dot(conv_ref[...], wo_ref[sb:, :], preferred_element_type=F32)
    hn = _rmsnorm(h, g_ref[...]).astype(BF16)
    gate = jnp.dot(hn, wg_ref[...], preferred_element_type=F32)
    up = jnp.dot(hn, wu_ref[...], preferred_element_type=F32)
    act = (gate * jax.nn.sigmoid(gate) * up).astype(BF16)
    h = h + jnp.dot(act, wd_ref[...], preferred_element_type=F32)
    if final_norm:
        h = _rmsnorm(h, fg_ref[...])
    o_ref[...] = h


def _mix_ffn(h2d, attn2d, conv2d, wo, g, wg, wu, wd, final_g, final_norm):
    rows, d = h2d.shape
    tm = _row_tile(rows, 512)
    row_spec = lambda w: pl.BlockSpec((tm, w), lambda i: (i, 0))
    return pl.pallas_call(
        functools.partial(_mix_ffn_kernel, final_norm=final_norm),
        out_shape=jax.ShapeDtypeStruct((rows, d), F32),
        grid=(rows // tm,),
        in_specs=[
            row_spec(d), row_spec(attn2d.shape[1]), row_spec(conv2d.shape[1]),
            _const_spec(wo.shape), _const_spec((1, d)),
            _const_spec(wg.shape), _const_spec(wu.shape), _const_spec(wd.shape),
            _const_spec((1, d)),
        ],
        out_specs=row_spec(d),
        compiler_params=pltpu.CompilerParams(
            dimension_semantics=("parallel",), vmem_limit_bytes=VMEM_LIMIT),
        name="mix_ffn",
    )(h2d, attn2d, conv2d, wo, g.reshape(1, d), wg, wu, wd, final_g.reshape(1, d))


def kernel(x, meta_tokens, mix_norm_g, w_in, conv_dw_w, conv_dw_b, conv_ln_g, conv_ln_b,
           w_out, ffn_norm_g, w_gate, w_up, w_down, final_norm_g):
    b, seq, d = x.shape
    n_meta = meta_tokens.shape[0]
    depth = w_in.shape[0]
    conv_ch = conv_dw_w.shape[-1]
    l = n_meta + seq
    lp = -(-l // LANES) * LANES

    meta = jnp.broadcast_to(meta_tokens.astype(x.dtype)[None], (b, n_meta, d))
    h = jnp.concatenate([meta, x, jnp.zeros((b, lp - l, d), x.dtype)], axis=1)
    h = h.reshape(b * lp, d)

    for i in range(depth):
        w = w_in[i].astype(BF16)
        w_qvag = jnp.concatenate([w[:, :SB_WIDTH], w[:, 2 * SB_WIDTH:]], axis=1)
        w_kt = w[:, SB_WIDTH:2 * SB_WIDTH].T
        q, kt, v, ca, cg = _in_proj(h, mix_norm_g[i], w_qvag, w_kt, conv_ch)
        seq3 = lambda t: t.reshape(b, lp, t.shape[-1])
        attn = _attention(seq3(q), kt, seq3(v))
        conv = _conformer_conv(seq3(ca), seq3(cg), conv_dw_w[i], conv_dw_b[i],
                               conv_ln_g[i], conv_ln_b[i])
        h = _mix_ffn(h, attn.reshape(b * lp, -1), conv.reshape(b * lp, -1),
                     w_out[i].astype(BF16), ffn_norm_g[i], w_gate[i].astype(BF16),
                     w_up[i].astype(BF16), w_down[i].astype(BF16), final_norm_g,
                     final_norm=(i == depth - 1))
    return h.reshape(b, lp, d)[:, n_meta:l]
```

```python
import functools
import math

import jax
import jax.numpy as jnp
from jax import lax
from jax.experimental import pallas as pl
from jax.experimental.pallas import tpu as pltpu

F32 = jnp.float32
BF16 = jnp.bfloat16

EPS = 1e-6
N_HEADS = 8
HEAD_DIM = 64
SB_WIDTH = N_HEADS * HEAD_DIM
LANES = 128
HEADS_PER_BLOCK = LANES // HEAD_DIM
EXP_ZERO_CUT = -104.0
LOG_ZERO = -1e30
BAND = 3
TILES = BAND * HEADS_PER_BLOCK
VMEM_LIMIT = 56 * 1024 * 1024


def _row_tile(rows, target):
    best = 8
    for t in range(8, min(rows, target) + 1, 8):
        if rows % t == 0:
            best = t
    return best


def _const_spec(shape):
    zeros = (0,) * len(shape)
    return pl.BlockSpec(shape, lambda *_: zeros, pipeline_mode=pl.Buffered(1))


def _rmsnorm(x, g):
    ms = jnp.mean(x * x, axis=-1, keepdims=True)
    return x * lax.rsqrt(ms + EPS) * g


def _in_proj_kernel(h_ref, g_ref, w_ref, wkt_ref, q_ref, kt_ref, v_ref, u_ref, *, q_scale):
    hn = _rmsnorm(h_ref[...], g_ref[...]).astype(BF16)
    sb = q_ref.shape[-1]
    c = u_ref.shape[-1]
    proj = lambda lo, width: jnp.dot(hn, w_ref[:, lo:lo + width], preferred_element_type=F32)
    q_ref[...] = (proj(0, sb) * q_scale).astype(q_ref.dtype)
    v_ref[...] = proj(sb, sb).astype(v_ref.dtype)
    u_ref[...] = proj(2 * sb, c) * jax.nn.sigmoid(proj(2 * sb + c, c))
    kt = lax.dot_general(wkt_ref[...], hn, (((1,), (1,)), ((), ())), preferred_element_type=F32)
    kt_ref[...] = kt.astype(kt_ref.dtype)


def _in_proj(h2d, g, w_qvag, w_kt, conv_ch):
    rows, d = h2d.shape
    tm = _row_tile(rows, 1024)
    row_out = lambda w, dt: (jax.ShapeDtypeStruct((rows, w), dt),
                             pl.BlockSpec((tm, w), lambda i: (i, 0)))
    outs = [row_out(SB_WIDTH, BF16),
            (jax.ShapeDtypeStruct((SB_WIDTH, rows), BF16),
             pl.BlockSpec((SB_WIDTH, tm), lambda i: (0, i))),
            row_out(SB_WIDTH, BF16), row_out(conv_ch, F32)]
    return pl.pallas_call(
        functools.partial(_in_proj_kernel, q_scale=1.0 / math.sqrt(HEAD_DIM)),
        out_shape=[o[0] for o in outs],
        grid=(rows // tm,),
        in_specs=[
            pl.BlockSpec((tm, d), lambda i: (i, 0)),
            _const_spec((1, d)),
            _const_spec(w_qvag.shape),
            _const_spec(w_kt.shape),
        ],
        out_specs=[o[1] for o in outs],
        compiler_params=pltpu.CompilerParams(
            dimension_semantics=("parallel",), vmem_limit_bytes=VMEM_LIMIT),
        name="in_proj",
    )(h2d, g.reshape(1, d), w_qvag, w_kt)


def _log_terms(z, mask):
    softplus = jnp.maximum(z, 0.0) + jnp.log(1.0 + jnp.exp(-jnp.abs(z)))
    log_1m_beta = -softplus
    log_beta = z - softplus
    if mask is not None:
        log_1m_beta = jnp.where(mask, log_1m_beta, 0.0)
        log_beta = jnp.where(mask, log_beta, LOG_ZERO)
    return log_1m_beta, log_beta


def _hi_lo(x):
    hi = x.astype(BF16)
    return hi, (x - hi.astype(F32)).astype(BF16)


def _attn_kernel(q_ref, kt_ref, v_ref, csw_ref, o_ref,
                 kst_ref, vst_ref, hilo0, hilo1, lb0, lb1, ws0, ws1, acc_ref, carry_ref):
    n_blocks = q_ref.shape[1] // LANES
    hilo_s, lb_s, ws_s = (hilo0, hilo1), (lb0, lb1), (ws0, ws1)
    lane = lax.broadcasted_iota(jnp.int32, (LANES, LANES), 1)
    row = lax.broadcasted_iota(jnp.int32, (LANES, LANES), 0)
    diag_mask = lane < row
    zero_half = jnp.zeros((HEAD_DIM, LANES), BF16)
    zero_blk = jnp.zeros((LANES, LANES), BF16)

    def rows(i, n=LANES):
        if isinstance(i, int):
            return pl.ds(i * n, n)
        return pl.ds(pl.multiple_of(i * n, n), n)

    for j in range(BAND - 1):
        kst_ref[j] = jnp.zeros(kst_ref.shape[1:], BF16)
        vst_ref[rows(j, 2 * LANES), :] = jnp.zeros((2 * LANES, LANES), BF16)
    for j in range(n_blocks):
        kt = kt_ref[:, j * LANES:(j + 1) * LANES]
        top = jnp.concatenate([kt[:HEAD_DIM], zero_half], axis=1)
        bottom = jnp.concatenate([zero_half, kt[HEAD_DIM:]], axis=1)
        kst_ref[j + BAND - 1] = jnp.concatenate([top, bottom], axis=0)

    def stack_v(j, _):
        v = v_ref[0, rows(j), :]
        base = pl.multiple_of((j + BAND - 1) * 2 * LANES, 2 * LANES)
        vst_ref[pl.ds(base, LANES), :] = jnp.where(lane < HEAD_DIM, v, zero_blk)
        vst_ref[pl.ds(base + LANES, LANES), :] = jnp.where(lane < HEAD_DIM, zero_blk, v)
        return 0

    lax.fori_loop(0, n_blocks, stack_v, 0)

    def stage_logits(i, s, first_valid=0):
        q2 = q_ref[0, rows(i), :]
        for p in range(BAND):
            for h in range(HEADS_PER_BLOCK):
                t = HEADS_PER_BLOCK * p + h
                if p < first_valid:
                    hilo_s[s][rows(t), :] = jnp.zeros((LANES, 2 * LANES), BF16)
                    lb_s[s][rows(t), :] = jnp.full((LANES, LANES), LOG_ZERO, F32)
            if p < first_valid:
                continue
            z2 = jnp.dot(q2, kst_ref[i + p], preferred_element_type=F32)
            for h in range(HEADS_PER_BLOCK):
                t = HEADS_PER_BLOCK * p + h
                log_1m_beta, log_beta = _log_terms(z2[:, h * LANES:(h + 1) * LANES],
                                                   diag_mask if p == BAND - 1 else None)
                hi, lo = _hi_lo(log_1m_beta)
                hilo_s[s][rows(t), 0:LANES] = hi
                hilo_s[s][rows(t), LANES:] = lo
                lb_s[s][rows(t), :] = log_beta

    def stage_weights(s):
        sums = jnp.dot(hilo_s[s][...], csw_ref[...], preferred_element_type=F32)
        worst = None
        for h in range(HEADS_PER_BLOCK):
            carry = jnp.zeros((LANES, LANES), F32)
            for p in reversed(range(BAND)):
                t = HEADS_PER_BLOCK * p + h
                blk = slice(t * LANES, (t + 1) * LANES)
                e = lb_s[s][blk, :] + sums[blk, :LANES] + carry
                ws_s[s][:, blk] = jnp.exp(e).astype(BF16)
                carry = carry + sums[blk, LANES:]
            worst = carry if worst is None else jnp.maximum(worst, carry)
        return worst

    def stage_output(i, s):
        start = i * 2 * LANES
        if not isinstance(i, int):
            start = pl.multiple_of(start, 2 * LANES)
        v_band = vst_ref[pl.ds(start, BAND * 2 * LANES), :]
        out = jnp.dot(ws_s[s][...], v_band, preferred_element_type=F32)
        o_ref[0, rows(i), :] = out.astype(o_ref.dtype)

    def step(i, s, worst):
        stage_logits(i, s)
        w = stage_weights(1 - s)
        stage_output(i - 2, s)
        return jnp.where(i - 1 >= BAND, jnp.maximum(worst, w), worst)

    worst = jnp.full((LANES, LANES), -jnp.inf, F32)
    n_pro = min(BAND - 1, n_blocks)
    for i in range(n_pro):
        stage_logits(i, i % 2, first_valid=BAND - 1 - i)
        if i >= 1:
            stage_weights((i - 1) % 2)
        if i >= 2:
            stage_output(i - 2, i % 2)
    n_main = n_blocks - n_pro
    if n_main > 0:
        assert n_pro % 2 == 0

        def pair(t, worst):
            i = n_pro + 2 * t
            worst = step(i, 0, worst)
            return step(i + 1, 1, worst)

        worst = lax.fori_loop(0, n_main // 2, pair, worst)
        if n_main % 2:
            worst = step(n_blocks - 1, (n_blocks - 1) % 2, worst)
    last = n_blocks - 1
    w = stage_weights(last % 2)
    if last >= BAND:
        worst = jnp.maximum(worst, w)
    if last >= 1:
        stage_output(last - 1, (last - 1) % 2)
    stage_output(last, last % 2)

    @pl.when(jnp.max(worst) > EXP_ZERO_CUT)
    def _():
        def q_block(i, _):
            q2 = q_ref[0, rows(i), :]
            q_pos = i * LANES + row
            acc_ref[...] = jnp.zeros_like(acc_ref)
            carry_ref[...] = jnp.zeros_like(carry_ref)

            def cond(state):
                j, alive = state
                return jnp.logical_and(j >= 0, alive > 0)

            def body(state):
                j, _ = state
                mask = (j * LANES + lane) < q_pos
                z2 = jnp.dot(q2, kst_ref[j + BAND - 1], preferred_element_type=F32)
                ws = []
                alive = None
                for h in range(HEADS_PER_BLOCK):
                    log_1m_beta, log_beta = _log_terms(z2[:, h * LANES:(h + 1) * LANES], mask)
                    sums = jnp.dot(jnp.concatenate(_hi_lo(log_1m_beta), axis=1), csw_ref[...],
                                   preferred_element_type=F32)
                    carry = carry_ref[h]
                    ws.append(jnp.exp(log_beta + sums[:, :LANES] + carry).astype(BF16))
                    carry = carry + sums[:, LANES:]
                    carry_ref[h] = carry
                    alive = carry if alive is None else jnp.maximum(alive, carry)
                base = pl.multiple_of((j + BAND - 1) * 2 * LANES, 2 * LANES)
                acc_ref[...] += jnp.dot(jnp.concatenate(ws, axis=1),
                                        vst_ref[pl.ds(base, 2 * LANES), :],
                                        preferred_element_type=F32)
                return j - 1, (jnp.max(alive) > EXP_ZERO_CUT).astype(jnp.int32)

            lax.while_loop(cond, body, (i, jnp.int32(1)))
            o_ref[0, rows(i), :] = acc_ref[...].astype(o_ref.dtype)
            return 0

        lax.fori_loop(0, n_blocks, q_block, 0)


def _cumsum_weights():
    j = jnp.arange(LANES)[:, None]
    s = jnp.arange(LANES)[None, :]
    half = jnp.concatenate([(j > s).astype(BF16), jnp.ones((LANES, LANES), BF16)], axis=1)
    return jnp.concatenate([half, half], axis=0)


def _attention(q, kt, v):
    b, lp, width = q.shape
    n_blocks = lp // LANES
    spec = pl.BlockSpec((1, lp, LANES), lambda bi, hi: (bi, 0, hi))
    stage = lambda shape, dt: [pltpu.VMEM(shape, dt), pltpu.VMEM(shape, dt)]
    return pl.pallas_call(
        _attn_kernel,
        out_shape=jax.ShapeDtypeStruct((b, lp, width), BF16),
        grid=(b, width // LANES),
        in_specs=[spec, pl.BlockSpec((LANES, lp), lambda bi, hi: (hi, bi)), spec,
                  _const_spec((2 * LANES, 2 * LANES))],
        out_specs=spec,
        scratch_shapes=[
            pltpu.VMEM((n_blocks + BAND - 1, LANES, 2 * LANES), BF16),
            pltpu.VMEM(((n_blocks + BAND - 1) * 2 * LANES, LANES), BF16),
            *stage((TILES * LANES, 2 * LANES), BF16),
            *stage((TILES * LANES, LANES), F32),
            *stage((LANES, TILES * LANES), BF16),
            pltpu.VMEM((LANES, LANES), F32),
            pltpu.VMEM((HEADS_PER_BLOCK, LANES, LANES), F32),
        ],
        compiler_params=pltpu.CompilerParams(
            dimension_semantics=("parallel", "parallel"), vmem_limit_bytes=VMEM_LIMIT),
        name="sb_attention",
    )(q, kt, v, _cumsum_weights())


CONV_HALO = 32
CONV_CHUNK = 64
CONV_TILE_ROWS = 1056


def _conv_kernel(u_ref, tail_ref, w_ref, b_ref, o_ref, *, n_taps):
    tl = u_ref.shape[1]
    w = w_ref[...]
    bias = b_ref[...]
    history = jnp.where(pl.program_id(1) > 0, tail_ref[0], 0.0)

    def convolve(window):
        acc = jnp.broadcast_to(bias, (CONV_CHUNK, bias.shape[1]))
        for r in range(8):
            shifted = window if r == 0 else pltpu.roll(window, r, 0)
            for a8 in range(0, n_taps, 8):
                s = a8 + r
                if s >= n_taps:
                    continue
                tap = n_taps - 1 - s
                lo = CONV_HALO - a8
                acc = acc + w[tap:tap + 1, :] * shifted[lo:lo + CONV_CHUNK, :]
        return acc

    o_ref[0, 0:CONV_CHUNK, :] = convolve(
        jnp.concatenate([history, u_ref[0, 0:CONV_CHUNK, :]], axis=0))

    def chunk(c, _):
        base = pl.multiple_of(c * CONV_CHUNK, 8)
        window = u_ref[0, pl.ds(base - CONV_HALO, CONV_CHUNK + CONV_HALO), :]
        o_ref[0, pl.ds(base, CONV_CHUNK), :] = convolve(window)
        return 0

    lax.fori_loop(1, tl // CONV_CHUNK, chunk, 0)


def _conformer_conv(u, dw_w, dw_b):
    b, lp, c = u.shape
    n_taps = dw_w.shape[0]
    assert n_taps - 1 <= CONV_HALO and lp % CONV_CHUNK == 0
    tl = CONV_CHUNK * max(n for n in range(1, lp // CONV_CHUNK + 1)
                          if (lp // CONV_CHUNK) % n == 0 and n * CONV_CHUNK <= CONV_TILE_ROWS)
    per_tile = tl // CONV_HALO
    tile = pl.BlockSpec((1, tl, c), lambda bi, ti: (bi, ti, 0))
    tail = pl.BlockSpec((1, CONV_HALO, c),
                        lambda bi, ti: (bi, jnp.maximum(ti * per_tile - 1, 0), 0))
    return pl.pallas_call(
        functools.partial(_conv_kernel, n_taps=n_taps),
        out_shape=jax.ShapeDtypeStruct((b, lp, c), F32),
        grid=(b, lp // tl),
        in_specs=[tile, tail, _const_spec((n_taps, c)), _const_spec((1, c))],
        out_specs=tile,
        compiler_params=pltpu.CompilerParams(
            dimension_semantics=("parallel", "parallel"), vmem_limit_bytes=VMEM_LIMIT),
        name="conformer_conv",
    )(u, u, dw_w, dw_b.reshape(1, c))


def _mix_ffn_kernel(h_ref, attn_ref, conv_ref, lng_ref, lnb_ref, wo_ref, g_ref, wg_ref, wu_ref,
                    wd_ref, fg_ref, o_ref, *, final_norm):
    sb = attn_ref.shape[-1]
    x = conv_ref[...]
    cen = x - jnp.mean(x, axis=-1, keepdims=True)
    var = jnp.mean(cen * cen, axis=-1, keepdims=True)
    y = cen * lax.rsqrt(var + EPS) * lng_ref[...] + lnb_ref[...]
    conv = (y * jax.nn.sigmoid(y)).astype(BF16)
    h = h_ref[...]
    h = h + jnp.dot(attn_ref[...], wo_ref[0:sb, :], preferred_element_type=F32)
    h = h + jnp.dot(conv, wo_ref[sb:, :], preferred_element_type=F32)
    hn = _rmsnorm(h, g_ref[...]).astype(BF16)
    gate = jnp.dot(hn, wg_ref[...], preferred_element_type=F32)
    up = jnp.dot(hn, wu_ref[...], preferred_element_type=F32)
    act = (gate * jax.nn.sigmoid(gate) * up).astype(BF16)
    h = h + jnp.dot(act, wd_ref[...], preferred_element_type=F32)
    if final_norm:
        h = _rmsnorm(h, fg_ref[...])
    o_ref[...] = h


def _mix_ffn(h2d, attn2d, conv2d, ln_g, ln_b, wo, g, wg, wu, wd, final_g, final_norm):
    rows, d = h2d.shape
    c = conv2d.shape[1]
    tm = _row_tile(rows, 512)
    row_spec = lambda w: pl.BlockSpec((tm, w), lambda i: (i, 0))
    vec = lambda v: v.reshape(1, -1)
    return pl.pallas_call(
        functools.partial(_mix_ffn_kernel, final_norm=final_norm),
        out_shape=jax.ShapeDtypeStruct((rows, d), F32),
        grid=(rows // tm,),
        in_specs=[
            row_spec(d), row_spec(attn2d.shape[1]), row_spec(c),
            _const_spec((1, c)), _const_spec((1, c)),
            _const_spec(wo.shape), _const_spec((1, d)),
            _const_spec(wg.shape), _const_spec(wu.shape), _const_spec(wd.shape),
            _const_spec((1, d)),
        ],
        out_specs=row_spec(d),
        compiler_params=pltpu.CompilerParams(
            dimension_semantics=("parallel",), vmem_limit_bytes=VMEM_LIMIT),
        name="mix_ffn",
    )(h2d, attn2d, conv2d, vec(ln_g), vec(ln_b), wo, vec(g), wg, wu, wd, vec(final_g))


def kernel(x, meta_tokens, mix_norm_g, w_in, conv_dw_w, conv_dw_b, conv_ln_g, conv_ln_b,
           w_out, ffn_norm_g, w_gate, w_up, w_down, final_norm_g):
    b, seq, d = x.shape
    n_meta = meta_tokens.shape[0]
    depth = w_in.shape[0]
    conv_ch = conv_dw_w.shape[-1]
    l = n_meta + seq
    lp = -(-l // LANES) * LANES

    meta = jnp.broadcast_to(meta_tokens.astype(x.dtype)[None], (b, n_meta, d))
    h = jnp.concatenate([meta, x, jnp.zeros((b, lp - l, d), x.dtype)], axis=1)
    h = h.reshape(b * lp, d)

    for i in range(depth):
        w = w_in[i].astype(BF16)
        w_qvag = jnp.concatenate([w[:, :SB_WIDTH], w[:, 2 * SB_WIDTH:]], axis=1)
        w_kt = w[:, SB_WIDTH:2 * SB_WIDTH].T
        q, kt, v, u = _in_proj(h, mix_norm_g[i], w_qvag, w_kt, conv_ch)
        seq3 = lambda t: t.reshape(b, lp, t.shape[-1])
        attn = _attention(seq3(q), kt, seq3(v))
        conv = _conformer_conv(seq3(u), conv_dw_w[i], conv_dw_b[i])
        h = _mix_ffn(h, attn.reshape(b * lp, -1), conv.reshape(b * lp, -1),
                     conv_ln_g[i], conv_ln_b[i], w_out[i].astype(BF16),
                     ffn_norm_g[i], w_gate[i].astype(BF16),
                     w_up[i].astype(BF16), w_down[i].astype(BF16), final_norm_g,
                     final_norm=(i == depth - 1))
    return h.reshape(b, lp, d)[:, n_meta:l]
```

```python
import functools
import math

import jax
import jax.numpy as jnp
from jax import lax
from jax.experimental import pallas as pl
from jax.experimental.pallas import tpu as pltpu

F32 = jnp.float32
BF16 = jnp.bfloat16

EPS = 1e-6
N_HEADS = 8
HEAD_DIM = 64
SB_WIDTH = N_HEADS * HEAD_DIM
LANES = 128
HEADS_PER_BLOCK = LANES // HEAD_DIM
EXP_ZERO_CUT = -104.0
LOG_ZERO = -1e30
BAND = 3
TILES = BAND * HEADS_PER_BLOCK
FAR_ROWS = 64
STAGED_ROWS = HEADS_PER_BLOCK * (FAR_ROWS + (BAND - 1) * LANES)
VMEM_LIMIT = 56 * 1024 * 1024


def _row_tile(rows, target):
    best = 8
    for t in range(8, min(rows, target) + 1, 8):
        if rows % t == 0:
            best = t
    return best


def _const_spec(shape):
    zeros = (0,) * len(shape)
    return pl.BlockSpec(shape, lambda *_: zeros, pipeline_mode=pl.Buffered(1))


def _rmsnorm(x, g):
    ms = jnp.mean(x * x, axis=-1, keepdims=True)
    return x * lax.rsqrt(ms + EPS) * g


def _in_proj_kernel(h_ref, g_ref, w_ref, wkt_ref, q_ref, kt_ref, v_ref, u_ref, *, q_scale):
    hn = _rmsnorm(h_ref[...], g_ref[...]).astype(BF16)
    sb = q_ref.shape[-1]
    c = u_ref.shape[-1]
    proj = lambda lo, width: jnp.dot(hn, w_ref[:, lo:lo + width], preferred_element_type=F32)
    q_ref[...] = (proj(0, sb) * q_scale).astype(q_ref.dtype)
    v_ref[...] = proj(sb, sb).astype(v_ref.dtype)
    u_ref[...] = proj(2 * sb, c) * jax.nn.sigmoid(proj(2 * sb + c, c))
    kt = lax.dot_general(wkt_ref[...], hn, (((1,), (1,)), ((), ())), preferred_element_type=F32)
    kt_ref[...] = kt.astype(kt_ref.dtype)


def _in_proj(h2d, g, w_qvag, w_kt, conv_ch):
    rows, d = h2d.shape
    tm = _row_tile(rows, 1024)
    row_out = lambda w, dt: (jax.ShapeDtypeStruct((rows, w), dt),
                             pl.BlockSpec((tm, w), lambda i: (i, 0)))
    outs = [row_out(SB_WIDTH, BF16),
            (jax.ShapeDtypeStruct((SB_WIDTH, rows), BF16),
             pl.BlockSpec((SB_WIDTH, tm), lambda i: (0, i))),
            row_out(SB_WIDTH, BF16), row_out(conv_ch, F32)]
    return pl.pallas_call(
        functools.partial(_in_proj_kernel, q_scale=1.0 / math.sqrt(HEAD_DIM)),
        out_shape=[o[0] for o in outs],
        grid=(rows // tm,),
        in_specs=[
            pl.BlockSpec((tm, d), lambda i: (i, 0)),
            _const_spec((1, d)),
            _const_spec(w_qvag.shape),
            _const_spec(w_kt.shape),
        ],
        out_specs=[o[1] for o in outs],
        compiler_params=pltpu.CompilerParams(
            dimension_semantics=("parallel",), vmem_limit_bytes=VMEM_LIMIT),
        name="in_proj",
    )(h2d, g.reshape(1, d), w_qvag, w_kt)


def _log_terms(z, mask):
    softplus = jnp.maximum(z, 0.0) + jnp.log(1.0 + jnp.exp(-jnp.abs(z)))
    log_1m_beta = -softplus
    log_beta = z - softplus
    if mask is not None:
        log_1m_beta = jnp.where(mask, log_1m_beta, 0.0)
        log_beta = jnp.where(mask, log_beta, LOG_ZERO)
    return log_1m_beta, log_beta


def _hi_lo(x):
    hi = x.astype(BF16)
    return hi, (x - hi.astype(F32)).astype(BF16)


def _attn_kernel(q_ref, kt_ref, v_ref, csw_ref, o_ref,
                 kst_ref, vst_ref, hilo0, hilo1, lb0, lb1, ws0, ws1, acc_ref, carry_ref):
    n_blocks = q_ref.shape[1] // LANES
    hilo_s, lb_s, ws_s = (hilo0, hilo1), (lb0, lb1), (ws0, ws1)
    lane = lax.broadcasted_iota(jnp.int32, (LANES, LANES), 1)
    row = lax.broadcasted_iota(jnp.int32, (LANES, LANES), 0)
    diag_mask = lane < row
    zero_half = jnp.zeros((HEAD_DIM, LANES), BF16)
    zero_blk = jnp.zeros((LANES, LANES), BF16)

    def rows(i, n=LANES):
        if isinstance(i, int):
            return pl.ds(i * n, n)
        return pl.ds(pl.multiple_of(i * n, n), n)

    for j in range(BAND - 1):
        kst_ref[j] = jnp.zeros(kst_ref.shape[1:], BF16)
        vst_ref[rows(j, 2 * LANES), :] = jnp.zeros((2 * LANES, LANES), BF16)
    for j in range(n_blocks):
        kt = kt_ref[:, j * LANES:(j + 1) * LANES]
        top = jnp.concatenate([kt[:HEAD_DIM], zero_half], axis=1)
        bottom = jnp.concatenate([zero_half, kt[HEAD_DIM:]], axis=1)
        kst_ref[j + BAND - 1] = jnp.concatenate([top, bottom], axis=0)

    def stack_v(j, _):
        v = v_ref[0, rows(j), :]
        base = pl.multiple_of((j + BAND - 1) * 2 * LANES, 2 * LANES)
        vst_ref[pl.ds(base, LANES), :] = jnp.where(lane < HEAD_DIM, v, zero_blk)
        vst_ref[pl.ds(base + LANES, LANES), :] = jnp.where(lane < HEAD_DIM, zero_blk, v)
        return 0

    lax.fori_loop(0, n_blocks, stack_v, 0)

    def tile_rows(p, h):
        if p == 0:
            return slice(h * FAR_ROWS, (h + 1) * FAR_ROWS)
        start = HEADS_PER_BLOCK * FAR_ROWS + (HEADS_PER_BLOCK * (p - 1) + h) * LANES
        return slice(start, start + LANES)

    for ws in ws_s:
        ws[...] = jnp.zeros(ws.shape, BF16)

    def stage_logits(i, s, first_valid=0):
        q2 = q_ref[0, rows(i), :]
        for p in range(BAND):
            n_rows = FAR_ROWS if p == 0 else LANES
            if p < first_valid:
                for h in range(HEADS_PER_BLOCK):
                    hilo_s[s][tile_rows(p, h), :] = jnp.zeros((n_rows, 2 * LANES), BF16)
                    lb_s[s][tile_rows(p, h), :] = jnp.full((n_rows, LANES), LOG_ZERO, F32)
                continue
            z2 = jnp.dot(q2[:n_rows], kst_ref[i + p], preferred_element_type=F32)
            for h in range(HEADS_PER_BLOCK):
                log_1m_beta, log_beta = _log_terms(z2[:, h * LANES:(h + 1) * LANES],
                                                   diag_mask if p == BAND - 1 else None)
                hi, lo = _hi_lo(log_1m_beta)
                hilo_s[s][tile_rows(p, h), 0:LANES] = hi
                hilo_s[s][tile_rows(p, h), LANES:] = lo
                lb_s[s][tile_rows(p, h), :] = log_beta

    def stage_weights(s):
        sums = jnp.dot(hilo_s[s][...], csw_ref[...], preferred_element_type=F32)
        left = None
        for h in range(HEADS_PER_BLOCK):
            carry = jnp.zeros((LANES, LANES), F32)
            for p in reversed(range(1, BAND)):
                t = HEADS_PER_BLOCK * p + h
                blk = tile_rows(p, h)
                e = lb_s[s][blk, :] + sums[blk, :LANES] + carry
                ws_s[s][:, t * LANES:(t + 1) * LANES] = jnp.exp(e).astype(BF16)
                carry = carry + sums[blk, LANES:]
            blk = tile_rows(0, h)
            e = lb_s[s][blk, :] + sums[blk, :LANES] + carry[:FAR_ROWS]
            ws_s[s][0:FAR_ROWS, h * LANES:(h + 1) * LANES] = jnp.exp(e).astype(BF16)
            carry = jnp.concatenate([carry[:FAR_ROWS] + sums[blk, LANES:], carry[FAR_ROWS:]],
                                    axis=0)
            left = carry if left is None else jnp.maximum(left, carry)
        return left

    def unseen_keys(i):
        return jnp.logical_or(i >= BAND, jnp.logical_and(i >= BAND - 1, row >= FAR_ROWS))

    def stage_output(i, s):
        start = i * 2 * LANES
        if not isinstance(i, int):
            start = pl.multiple_of(start, 2 * LANES)
        v_band = vst_ref[pl.ds(start, BAND * 2 * LANES), :]
        out = jnp.dot(ws_s[s][...], v_band, preferred_element_type=F32)
        o_ref[0, rows(i), :] = out.astype(o_ref.dtype)

    def step(i, s, worst):
        stage_logits(i, s)
        w = stage_weights(1 - s)
        stage_output(i - 2, s)
        return jnp.where(unseen_keys(i - 1), jnp.maximum(worst, w), worst)

    worst = jnp.full((LANES, LANES), -jnp.inf, F32)
    n_pro = min(BAND - 1, n_blocks)
    for i in range(n_pro):
        stage_logits(i, i % 2, first_valid=BAND - 1 - i)
        if i >= 1:
            stage_weights((i - 1) % 2)
        if i >= 2:
            stage_output(i - 2, i % 2)
    n_main = n_blocks - n_pro
    if n_main > 0:
        assert n_pro % 2 == 0

        def pair(t, worst):
            i = n_pro + 2 * t
            worst = step(i, 0, worst)
            return step(i + 1, 1, worst)

        worst = lax.fori_loop(0, n_main // 2, pair, worst)
        if n_main % 2:
            worst = step(n_blocks - 1, (n_blocks - 1) % 2, worst)
    last = n_blocks - 1
    w = stage_weights(last % 2)
    worst = jnp.where(unseen_keys(last), jnp.maximum(worst, w), worst)
    if last >= 1:
        stage_output(last - 1, (last - 1) % 2)
    stage_output(last, last % 2)

    @pl.when(jnp.max(worst) > EXP_ZERO_CUT)
    def _():
        def q_block(i, _):
            q2 = q_ref[0, rows(i), :]
            q_pos = i * LANES + row
            acc_ref[...] = jnp.zeros_like(acc_ref)
            carry_ref[...] = jnp.zeros_like(carry_ref)

            def cond(state):
                j, alive = state
                return jnp.logical_and(j >= 0, alive > 0)

            def body(state):
                j, _ = state
                mask = (j * LANES + lane) < q_pos
                z2 = jnp.dot(q2, kst_ref[j + BAND - 1], preferred_element_type=F32)
                ws = []
                alive = None
                for h in range(HEADS_PER_BLOCK):
                    log_1m_beta, log_beta = _log_terms(z2[:, h * LANES:(h + 1) * LANES], mask)
                    sums = jnp.dot(jnp.concatenate(_hi_lo(log_1m_beta), axis=1), csw_ref[...],
                                   preferred_element_type=F32)
                    carry = carry_ref[h]
                    ws.append(jnp.exp(log_beta + sums[:, :LANES] + carry).astype(BF16))
                    carry = carry + sums[:, LANES:]
                    carry_ref[h] = carry
                    alive = carry if alive is None else jnp.maximum(alive, carry)
                base = pl.multiple_of((j + BAND - 1) * 2 * LANES, 2 * LANES)
                acc_ref[...] += jnp.dot(jnp.concatenate(ws, axis=1),
                                        vst_ref[pl.ds(base, 2 * LANES), :],
                                        preferred_element_type=F32)
                return j - 1, (jnp.max(alive) > EXP_ZERO_CUT).astype(jnp.int32)

            lax.while_loop(cond, body, (i, jnp.int32(1)))
            o_ref[0, rows(i), :] = acc_ref[...].astype(o_ref.dtype)
            return 0

        lax.fori_loop(0, n_blocks, q_block, 0)


def _cumsum_weights():
    j = jnp.arange(LANES)[:, None]
    s = jnp.arange(LANES)[None, :]
    half = jnp.concatenate([(j > s).astype(BF16), jnp.ones((LANES, LANES), BF16)], axis=1)
    return jnp.concatenate([half, half], axis=0)


def _attention(q, kt, v):
    b, lp, width = q.shape
    n_blocks = lp // LANES
    spec = pl.BlockSpec((1, lp, LANES), lambda bi, hi: (bi, 0, hi))
    stage = lambda shape, dt: [pltpu.VMEM(shape, dt), pltpu.VMEM(shape, dt)]
    return pl.pallas_call(
        _attn_kernel,
        out_shape=jax.ShapeDtypeStruct((b, lp, width), BF16),
        grid=(b, width // LANES),
        in_specs=[spec, pl.BlockSpec((LANES, lp), lambda bi, hi: (hi, bi)), spec,
                  _const_spec((2 * LANES, 2 * LANES))],
        out_specs=spec,
        scratch_shapes=[
            pltpu.VMEM((n_blocks + BAND - 1, LANES, 2 * LANES), BF16),
            pltpu.VMEM(((n_blocks + BAND - 1) * 2 * LANES, LANES), BF16),
            *stage((STAGED_ROWS, 2 * LANES), BF16),
            *stage((STAGED_ROWS, LANES), F32),
            *stage((LANES, TILES * LANES), BF16),
            pltpu.VMEM((LANES, LANES), F32),
            pltpu.VMEM((HEADS_PER_BLOCK, LANES, LANES), F32),
        ],
        compiler_params=pltpu.CompilerParams(
            dimension_semantics=("parallel", "parallel"), vmem_limit_bytes=VMEM_LIMIT),
        name="sb_attention",
    )(q, kt, v, _cumsum_weights())


CONV_HALO = 32
CONV_CHUNK = 64
CONV_TILE_ROWS = 1056


def _conv_kernel(u_ref, tail_ref, w_ref, b_ref, o_ref, *, n_taps):
    tl = u_ref.shape[1]
    w = w_ref[...]
    bias = b_ref[...]
    history = jnp.where(pl.program_id(1) > 0, tail_ref[0], 0.0)

    def convolve(window):
        acc = jnp.broadcast_to(bias, (CONV_CHUNK, bias.shape[1]))
        for r in range(8):
            shifted = window if r == 0 else pltpu.roll(window, r, 0)
            for a8 in range(0, n_taps, 8):
                s = a8 + r
                if s >= n_taps:
                    continue
                tap = n_taps - 1 - s
                lo = CONV_HALO - a8
                acc = acc + w[tap:tap + 1, :] * shifted[lo:lo + CONV_CHUNK, :]
        return acc

    o_ref[0, 0:CONV_CHUNK, :] = convolve(
        jnp.concatenate([history, u_ref[0, 0:CONV_CHUNK, :]], axis=0))

    def chunk(c, _):
        base = pl.multiple_of(c * CONV_CHUNK, 8)
        window = u_ref[0, pl.ds(base - CONV_HALO, CONV_CHUNK + CONV_HALO), :]
        o_ref[0, pl.ds(base, CONV_CHUNK), :] = convolve(window)
        return 0

    lax.fori_loop(1, tl // CONV_CHUNK, chunk, 0)


def _conformer_conv(u, dw_w, dw_b):
    b, lp, c = u.shape
    n_taps = dw_w.shape[0]
    assert n_taps - 1 <= CONV_HALO and lp % CONV_CHUNK == 0
    tl = CONV_CHUNK * max(n for n in range(1, lp // CONV_CHUNK + 1)
                          if (lp // CONV_CHUNK) % n == 0 and n * CONV_CHUNK <= CONV_TILE_ROWS)
    per_tile = tl // CONV_HALO
    tile = pl.BlockSpec((1, tl, c), lambda bi, ti: (bi, ti, 0))
    tail = pl.BlockSpec((1, CONV_HALO, c),
                        lambda bi, ti: (bi, jnp.maximum(ti * per_tile - 1, 0), 0))
    return pl.pallas_call(
        functools.partial(_conv_kernel, n_taps=n_taps),
        out_shape=jax.ShapeDtypeStruct((b, lp, c), F32),
        grid=(b, lp // tl),
        in_specs=[tile, tail, _const_spec((n_taps, c)), _const_spec((1, c))],
        out_specs=tile,
        compiler_params=pltpu.CompilerParams(
            dimension_semantics=("parallel", "parallel"), vmem_limit_bytes=VMEM_LIMIT),
        name="conformer_conv",
    )(u, u, dw_w, dw_b.reshape(1, c))


def _mix_ffn_kernel(h_ref, attn_ref, conv_ref, lng_ref, lnb_ref, wo_ref, g_ref, wg_ref, wu_ref,
                    wd_ref, fg_ref, o_ref, *, final_norm):
    sb = attn_ref.shape[-1]
    x = conv_ref[...]
    cen = x - jnp.mean(x, axis=-1, keepdims=True)
    var = jnp.mean(cen * cen, axis=-1, keepdims=True)
    y = cen * lax.rsqrt(var + EPS) * lng_ref[...] + lnb_ref[...]
    conv = (y * jax.nn.sigmoid(y)).astype(BF16)
    h = h_ref[...]
    h = h + jnp.dot(attn_ref[...], wo_ref[0:sb, :], preferred_element_type=F32)
    h = h + jnp.dot(conv, wo_ref[sb:, :], preferred_element_type=F32)
    hn = _rmsnorm(h, g_ref[...]).astype(BF16)
    gate = jnp.dot(hn, wg_ref[...], preferred_element_type=F32)
    up = jnp.dot(hn, wu_ref[...], preferred_element_type=F32)
    act = (gate * jax.nn.sigmoid(gate) * up).astype(BF16)
    h = h + jnp.dot(act, wd_ref[...], preferred_element_type=F32)
    if final_norm:
        h = _rmsnorm(h, fg_ref[...])
    o_ref[...] = h


def _mix_ffn(h2d, attn2d, conv2d, ln_g, ln_b, wo, g, wg, wu, wd, final_g, final_norm,
             keep=None):
    rows, d = h2d.shape
    c = conv2d.shape[1]
    vec = lambda v: v.reshape(1, -1)
    if keep is None:
        tm = _row_tile(rows, 512)
        out_rows = rows
        in_spec = lambda w: pl.BlockSpec((tm, w), lambda i: (i, 0))
    else:
        lp, first, count = keep
        tm = _row_tile(count, 512)
        per_seq = count // tm
        out_rows = rows // lp * count
        in_spec = lambda w: pl.BlockSpec(
            (pl.Element(tm), pl.Element(w)),
            lambda i: (pl.multiple_of((i // per_seq) * lp + first + (i % per_seq) * tm, 16), 0))
    row_spec = lambda w: pl.BlockSpec((tm, w), lambda i: (i, 0))
    return pl.pallas_call(
        functools.partial(_mix_ffn_kernel, final_norm=final_norm),
        out_shape=jax.ShapeDtypeStruct((out_rows, d), F32),
        grid=(out_rows // tm,),
        in_specs=[
            in_spec(d), in_spec(attn2d.shape[1]), in_spec(c),
            _const_spec((1, c)), _const_spec((1, c)),
            _const_spec(wo.shape), _const_spec((1, d)),
            _const_spec(wg.shape), _const_spec(wu.shape), _const_spec(wd.shape),
            _const_spec((1, d)),
        ],
        out_specs=row_spec(d),
        compiler_params=pltpu.CompilerParams(
            dimension_semantics=("parallel",), vmem_limit_bytes=VMEM_LIMIT),
        name="mix_ffn",
    )(h2d, attn2d, conv2d, vec(ln_g), vec(ln_b), wo, vec(g), wg, wu, wd, vec(final_g))


def kernel(x, meta_tokens, mix_norm_g, w_in, conv_dw_w, conv_dw_b, conv_ln_g, conv_ln_b,
           w_out, ffn_norm_g, w_gate, w_up, w_down, final_norm_g):
    b, seq, d = x.shape
    n_meta = meta_tokens.shape[0]
    depth = w_in.shape[0]
    conv_ch = conv_dw_w.shape[-1]
    l = n_meta + seq
    lp = -(-l // LANES) * LANES

    meta = jnp.broadcast_to(meta_tokens.astype(x.dtype)[None], (b, n_meta, d))
    h = jnp.concatenate([meta, x, jnp.zeros((b, lp - l, d), x.dtype)], axis=1)
    h = h.reshape(b * lp, d)

    for i in range(depth):
        w = w_in[i].astype(BF16)
        w_qvag = jnp.concatenate([w[:, :SB_WIDTH], w[:, 2 * SB_WIDTH:]], axis=1)
        w_kt = w[:, SB_WIDTH:2 * SB_WIDTH].T
        q, kt, v, u = _in_proj(h, mix_norm_g[i], w_qvag, w_kt, conv_ch)
        seq3 = lambda t: t.reshape(b, lp, t.shape[-1])
        attn = _attention(seq3(q), kt, seq3(v))
        conv = _conformer_conv(seq3(u), conv_dw_w[i], conv_dw_b[i])
        last = i == depth - 1
        keep = (lp, n_meta, seq) if last and n_meta % 16 == 0 and seq % 16 == 0 else None
        h = _mix_ffn(h, attn.reshape(b * lp, -1), conv.reshape(b * lp, -1),
                     conv_ln_g[i], conv_ln_b[i], w_out[i].astype(BF16),
                     ffn_norm_g[i], w_gate[i].astype(BF16),
                     w_up[i].astype(BF16), w_down[i].astype(BF16), final_norm_g,
                     final_norm=last, keep=keep)
    if keep is not None:
        return h.reshape(b, seq, d)
    return h.reshape(b, lp, d)[:, n_meta:l]
```

```python
import functools
import math

import jax
import jax.numpy as jnp
from jax import lax
from jax.experimental import pallas as pl
from jax.experimental.pallas import tpu as pltpu

F32 = jnp.float32
BF16 = jnp.bfloat16

EPS = 1e-6
N_HEADS = 8
HEAD_DIM = 64
SB_WIDTH = N_HEADS * HEAD_DIM
LANES = 128
HEADS_PER_BLOCK = LANES // HEAD_DIM
EXP_ZERO_CUT = -104.0
LOG_ZERO = -1e30
BAND = 3
TILES = BAND * HEADS_PER_BLOCK
FAR_ROWS = 64
STAGED_ROWS = HEADS_PER_BLOCK * (FAR_ROWS + (BAND - 1) * LANES)
VMEM_LIMIT = 56 * 1024 * 1024


def _row_tile(rows, target):
    best = 8
    for t in range(8, min(rows, target) + 1, 8):
        if rows % t == 0:
            best = t
    return best


def _const_spec(shape):
    zeros = (0,) * len(shape)
    return pl.BlockSpec(shape, lambda *_: zeros, pipeline_mode=pl.Buffered(1))


def _layer_spec(stacked, layer):
    zeros = (0,) * (stacked.ndim - 1)
    return pl.BlockSpec((None,) + stacked.shape[1:], lambda *_: (layer,) + zeros,
                        pipeline_mode=pl.Buffered(1))


def _rmsnorm(x, g):
    ms = jnp.mean(x * x, axis=-1, keepdims=True)
    return x * lax.rsqrt(ms + EPS) * g


def _in_proj_kernel(h_ref, g_ref, w_ref, wkt_ref, q_ref, kt_ref, v_ref, u_ref, *, q_scale):
    hn = _rmsnorm(h_ref[...], g_ref[...]).astype(BF16)
    sb = q_ref.shape[-1]
    c = u_ref.shape[-1]
    proj = lambda lo, width: jnp.dot(hn, w_ref[:, lo:lo + width], preferred_element_type=F32)
    q_ref[...] = (proj(0, sb) * q_scale).astype(q_ref.dtype)
    v_ref[...] = proj(2 * sb, sb).astype(v_ref.dtype)
    u_ref[...] = proj(3 * sb, c) * jax.nn.sigmoid(proj(3 * sb + c, c))
    kt = lax.dot_general(wkt_ref[...], hn, (((1,), (1,)), ((), ())), preferred_element_type=F32)
    kt_ref[...] = kt.astype(kt_ref.dtype)


def _in_proj(h2d, g, w_in, w_kt, layer, conv_ch):
    rows, d = h2d.shape
    tm = _row_tile(rows, 1024)
    row_out = lambda w, dt: (jax.ShapeDtypeStruct((rows, w), dt),
                             pl.BlockSpec((tm, w), lambda i: (i, 0)))
    outs = [row_out(SB_WIDTH, BF16),
            (jax.ShapeDtypeStruct((SB_WIDTH, rows), BF16),
             pl.BlockSpec((SB_WIDTH, tm), lambda i: (0, i))),
            row_out(SB_WIDTH, BF16), row_out(conv_ch, F32)]
    return pl.pallas_call(
        functools.partial(_in_proj_kernel, q_scale=1.0 / math.sqrt(HEAD_DIM)),
        out_shape=[o[0] for o in outs],
        grid=(rows // tm,),
        in_specs=[
            pl.BlockSpec((tm, d), lambda i: (i, 0)),
            _const_spec((1, d)),
            _layer_spec(w_in, layer),
            _layer_spec(w_kt, layer),
        ],
        out_specs=[o[1] for o in outs],
        compiler_params=pltpu.CompilerParams(
            dimension_semantics=("parallel",), vmem_limit_bytes=VMEM_LIMIT),
        name="in_proj",
    )(h2d, g.reshape(1, d), w_in, w_kt)


def _logit_terms(z, mask):
    softplus = jnp.maximum(z, 0.0) + jnp.log(1.0 + jnp.exp(-jnp.abs(z)))
    if mask is not None:
        softplus = jnp.where(mask, softplus, 0.0)
        z = jnp.where(mask, z, LOG_ZERO)
    return softplus, z


def _hi_lo(x):
    hi = x.astype(BF16)
    return hi, (x - hi.astype(F32)).astype(BF16)


def _attn_kernel(q_ref, kt_ref, v_ref, csw_ref, o_ref,
                 kst_ref, vst_ref, hilo0, hilo1, zst0, zst1, ws0, ws1, acc_ref, carry_ref):
    n_blocks = q_ref.shape[1] // LANES
    hilo_s, zst_s, ws_s = (hilo0, hilo1), (zst0, zst1), (ws0, ws1)
    lane = lax.broadcasted_iota(jnp.int32, (LANES, LANES), 1)
    row = lax.broadcasted_iota(jnp.int32, (LANES, LANES), 0)
    diag_mask = lane < row
    zero_half = jnp.zeros((HEAD_DIM, LANES), BF16)
    zero_blk = jnp.zeros((LANES, LANES), BF16)

    def rows(i, n=LANES):
        if isinstance(i, int):
            return pl.ds(i * n, n)
        return pl.ds(pl.multiple_of(i * n, n), n)

    for j in range(BAND - 1):
        kst_ref[j] = jnp.zeros(kst_ref.shape[1:], BF16)
        vst_ref[rows(j, 2 * LANES), :] = jnp.zeros((2 * LANES, LANES), BF16)
    for j in range(n_blocks):
        kt = kt_ref[:, j * LANES:(j + 1) * LANES]
        top = jnp.concatenate([kt[:HEAD_DIM], zero_half], axis=1)
        bottom = jnp.concatenate([zero_half, kt[HEAD_DIM:]], axis=1)
        kst_ref[j + BAND - 1] = jnp.concatenate([top, bottom], axis=0)

    def stack_v(j, _):
        v = v_ref[0, rows(j), :]
        base = pl.multiple_of((j + BAND - 1) * 2 * LANES, 2 * LANES)
        vst_ref[pl.ds(base, LANES), :] = jnp.where(lane < HEAD_DIM, v, zero_blk)
        vst_ref[pl.ds(base + LANES, LANES), :] = jnp.where(lane < HEAD_DIM, zero_blk, v)
        return 0

    lax.fori_loop(0, n_blocks, stack_v, 0)

    def tile_rows(p, h):
        if p == 0:
            return slice(h * FAR_ROWS, (h + 1) * FAR_ROWS)
        start = HEADS_PER_BLOCK * FAR_ROWS + (HEADS_PER_BLOCK * (p - 1) + h) * LANES
        return slice(start, start + LANES)

    for ws in ws_s:
        ws[...] = jnp.zeros(ws.shape, BF16)

    def stage_logits(i, s, first_valid=0):
        q2 = q_ref[0, rows(i), :]
        for p in range(BAND):
            n_rows = FAR_ROWS if p == 0 else LANES
            if p < first_valid:
                for h in range(HEADS_PER_BLOCK):
                    hilo_s[s][tile_rows(p, h), :] = jnp.zeros((n_rows, 2 * LANES), BF16)
                    zst_s[s][tile_rows(p, h), :] = jnp.full((n_rows, LANES), LOG_ZERO, F32)
                continue
            z2 = jnp.dot(q2[:n_rows], kst_ref[i + p], preferred_element_type=F32)
            for h in range(HEADS_PER_BLOCK):
                softplus, z = _logit_terms(z2[:, h * LANES:(h + 1) * LANES],
                                           diag_mask if p == BAND - 1 else None)
                hi, lo = _hi_lo(softplus)
                hilo_s[s][tile_rows(p, h), 0:LANES] = hi
                hilo_s[s][tile_rows(p, h), LANES:] = lo
                zst_s[s][tile_rows(p, h), :] = z

    def stage_weights(s):
        sums = jnp.dot(hilo_s[s][...], csw_ref[...], preferred_element_type=F32)
        left = None
        for h in range(HEADS_PER_BLOCK):
            carry = jnp.zeros((LANES, LANES), F32)
            for p in reversed(range(1, BAND)):
                t = HEADS_PER_BLOCK * p + h
                blk = tile_rows(p, h)
                e = zst_s[s][blk, :] + sums[blk, :LANES] + carry
                ws_s[s][:, t * LANES:(t + 1) * LANES] = jnp.exp(e).astype(BF16)
                carry = carry + sums[blk, LANES:]
            blk = tile_rows(0, h)
            e = zst_s[s][blk, :] + sums[blk, :LANES] + carry[:FAR_ROWS]
            ws_s[s][0:FAR_ROWS, h * LANES:(h + 1) * LANES] = jnp.exp(e).astype(BF16)
            carry = jnp.concatenate([carry[:FAR_ROWS] + sums[blk, LANES:], carry[FAR_ROWS:]],
                                    axis=0)
            left = carry if left is None else jnp.maximum(left, carry)
        return left

    def unseen_keys(i):
        return jnp.logical_or(i >= BAND, jnp.logical_and(i >= BAND - 1, row >= FAR_ROWS))

    def stage_output(i, s):
        start = i * 2 * LANES
        if not isinstance(i, int):
            start = pl.multiple_of(start, 2 * LANES)
        v_band = vst_ref[pl.ds(start, BAND * 2 * LANES), :]
        out = jnp.dot(ws_s[s][...], v_band, preferred_element_type=F32)
        o_ref[0, rows(i), :] = out.astype(o_ref.dtype)

    def step(i, s, worst):
        stage_logits(i, s)
        w = stage_weights(1 - s)
        stage_output(i - 2, s)
        return jnp.where(unseen_keys(i - 1), jnp.maximum(worst, w), worst)

    worst = jnp.full((LANES, LANES), -jnp.inf, F32)
    n_pro = min(BAND - 1, n_blocks)
    for i in range(n_pro):
        stage_logits(i, i % 2, first_valid=BAND - 1 - i)
        if i >= 1:
            stage_weights((i - 1) % 2)
        if i >= 2:
            stage_output(i - 2, i % 2)
    n_main = n_blocks - n_pro
    if n_main > 0:
        assert n_pro % 2 == 0

        def pair(t, worst):
            i = n_pro + 2 * t
            worst = step(i, 0, worst)
            return step(i + 1, 1, worst)

        worst = lax.fori_loop(0, n_main // 2, pair, worst)
        if n_main % 2:
            worst = step(n_blocks - 1, (n_blocks - 1) % 2, worst)
    last = n_blocks - 1
    w = stage_weights(last % 2)
    worst = jnp.where(unseen_keys(last), jnp.maximum(worst, w), worst)
    if last >= 1:
        stage_output(last - 1, (last - 1) % 2)
    stage_output(last, last % 2)

    @pl.when(jnp.max(worst) > EXP_ZERO_CUT)
    def _():
        def q_block(i, _):
            q2 = q_ref[0, rows(i), :]
            q_pos = i * LANES + row
            acc_ref[...] = jnp.zeros_like(acc_ref)
            carry_ref[...] = jnp.zeros_like(carry_ref)

            def cond(state):
                j, alive = state
                return jnp.logical_and(j >= 0, alive > 0)

            def body(state):
                j, _ = state
                mask = (j * LANES + lane) < q_pos
                z2 = jnp.dot(q2, kst_ref[j + BAND - 1], preferred_element_type=F32)
                ws = []
                alive = None
                for h in range(HEADS_PER_BLOCK):
                    softplus, z = _logit_terms(z2[:, h * LANES:(h + 1) * LANES], mask)
                    sums = jnp.dot(jnp.concatenate(_hi_lo(softplus), axis=1), csw_ref[...],
                                   preferred_element_type=F32)
                    carry = carry_ref[h]
                    ws.append(jnp.exp(z + sums[:, :LANES] + carry).astype(BF16))
                    carry = carry + sums[:, LANES:]
                    carry_ref[h] = carry
                    alive = carry if alive is None else jnp.maximum(alive, carry)
                base = pl.multiple_of((j + BAND - 1) * 2 * LANES, 2 * LANES)
                acc_ref[...] += jnp.dot(jnp.concatenate(ws, axis=1),
                                        vst_ref[pl.ds(base, 2 * LANES), :],
                                        preferred_element_type=F32)
                return j - 1, (jnp.max(alive) > EXP_ZERO_CUT).astype(jnp.int32)

            lax.while_loop(cond, body, (i, jnp.int32(1)))
            o_ref[0, rows(i), :] = acc_ref[...].astype(o_ref.dtype)
            return 0

        lax.fori_loop(0, n_blocks, q_block, 0)


def _cumsum_weights():
    j = jnp.arange(LANES)[:, None]
    s = jnp.arange(LANES)[None, :]
    half = -jnp.concatenate([(j >= s).astype(BF16), jnp.ones((LANES, LANES), BF16)], axis=1)
    return jnp.concatenate([half, half], axis=0)


def _attention(q, kt, v):
    b, lp, width = q.shape
    n_blocks = lp // LANES
    spec = pl.BlockSpec((1, lp, LANES), lambda bi, hi: (bi, 0, hi))
    stage = lambda shape, dt: [pltpu.VMEM(shape, dt), pltpu.VMEM(shape, dt)]
    return pl.pallas_call(
        _attn_kernel,
        out_shape=jax.ShapeDtypeStruct((b, lp, width), BF16),
        grid=(b, width // LANES),
        in_specs=[spec, pl.BlockSpec((LANES, lp), lambda bi, hi: (hi, bi)), spec,
                  _const_spec((2 * LANES, 2 * LANES))],
        out_specs=spec,
        scratch_shapes=[
            pltpu.VMEM((n_blocks + BAND - 1, LANES, 2 * LANES), BF16),
            pltpu.VMEM(((n_blocks + BAND - 1) * 2 * LANES, LANES), BF16),
            *stage((STAGED_ROWS, 2 * LANES), BF16),
            *stage((STAGED_ROWS, LANES), F32),
            *stage((LANES, TILES * LANES), BF16),
            pltpu.VMEM((LANES, LANES), F32),
            pltpu.VMEM((HEADS_PER_BLOCK, LANES, LANES), F32),
        ],
        compiler_params=pltpu.CompilerParams(
            dimension_semantics=("parallel", "parallel"), vmem_limit_bytes=VMEM_LIMIT),
        name="sb_attention",
    )(q, kt, v, _cumsum_weights())


CONV_HALO = 32
CONV_CHUNK = 64
CONV_TILE_ROWS = 1056


def _conv_kernel(u_ref, tail_ref, w_ref, b_ref, o_ref, *, n_taps):
    tl = u_ref.shape[1]
    w = w_ref[...]
    bias = b_ref[...]
    history = jnp.where(pl.program_id(1) > 0, tail_ref[0], 0.0)

    def convolve(window):
        acc = jnp.broadcast_to(bias, (CONV_CHUNK, bias.shape[1]))
        for r in range(8):
            shifted = window if r == 0 else pltpu.roll(window, r, 0)
            for a8 in range(0, n_taps, 8):
                s = a8 + r
                if s >= n_taps:
                    continue
                tap = n_taps - 1 - s
                lo = CONV_HALO - a8
                acc = acc + w[tap:tap + 1, :] * shifted[lo:lo + CONV_CHUNK, :]
        return acc

    o_ref[0, 0:CONV_CHUNK, :] = convolve(
        jnp.concatenate([history, u_ref[0, 0:CONV_CHUNK, :]], axis=0))

    def chunk(c, _):
        base = pl.multiple_of(c * CONV_CHUNK, 8)
        window = u_ref[0, pl.ds(base - CONV_HALO, CONV_CHUNK + CONV_HALO), :]
        o_ref[0, pl.ds(base, CONV_CHUNK), :] = convolve(window)
        return 0

    lax.fori_loop(1, tl // CONV_CHUNK, chunk, 0)


def _conformer_conv(u, dw_w, dw_b):
    b, lp, c = u.shape
    n_taps = dw_w.shape[0]
    assert n_taps - 1 <= CONV_HALO and lp % CONV_CHUNK == 0
    tl = CONV_CHUNK * max(n for n in range(1, lp // CONV_CHUNK + 1)
                          if (lp // CONV_CHUNK) % n == 0 and n * CONV_CHUNK <= CONV_TILE_ROWS)
    per_tile = tl // CONV_HALO
    tile = pl.BlockSpec((1, tl, c), lambda bi, ti: (bi, ti, 0))
    tail = pl.BlockSpec((1, CONV_HALO, c),
                        lambda bi, ti: (bi, jnp.maximum(ti * per_tile - 1, 0), 0))
    return pl.pallas_call(
        functools.partial(_conv_kernel, n_taps=n_taps),
        out_shape=jax.ShapeDtypeStruct((b, lp, c), F32),
        grid=(b, lp // tl),
        in_specs=[tile, tail, _const_spec((n_taps, c)), _const_spec((1, c))],
        out_specs=tile,
        compiler_params=pltpu.CompilerParams(
            dimension_semantics=("parallel", "parallel"), vmem_limit_bytes=VMEM_LIMIT),
        name="conformer_conv",
    )(u, u, dw_w, dw_b.reshape(1, c))


FFN_TILE_ROWS = 512


def _mix_ffn_kernel(h_ref, attn_ref, conv_ref, lng_ref, lnb_ref, wo_ref, g_ref, wg_ref, wu_ref,
                    wd_ref, fg_ref, o_ref, *, final_norm):
    sb = attn_ref.shape[-1]
    x = conv_ref[...]
    cen = x - jnp.mean(x, axis=-1, keepdims=True)
    var = jnp.mean(cen * cen, axis=-1, keepdims=True)
    y = cen * lax.rsqrt(var + EPS) * lng_ref[...] + lnb_ref[...]
    conv = (y * jax.nn.sigmoid(y)).astype(BF16)
    h = h_ref[...]
    h = h + jnp.dot(attn_ref[...], wo_ref[0:sb, :], preferred_element_type=F32)
    h = h + jnp.dot(conv, wo_ref[sb:, :], preferred_element_type=F32)
    hn = _rmsnorm(h, g_ref[...]).astype(BF16)
    gate = jnp.dot(hn, wg_ref[...], preferred_element_type=F32)
    up = jnp.dot(hn, wu_ref[...], preferred_element_type=F32)
    act = (gate * jax.nn.sigmoid(gate) * up).astype(BF16)
    h = h + jnp.dot(act, wd_ref[...], preferred_element_type=F32)
    if final_norm:
        h = _rmsnorm(h, fg_ref[...])
    o_ref[...] = h


def _mix_ffn(h2d, attn2d, conv2d, ln_g, ln_b, wo, g, wg, wu, wd, final_g, layer, final_norm,
             keep=None):
    rows, d = h2d.shape
    c = conv2d.shape[1]
    vec = lambda v: v.reshape(1, -1)
    if keep is None:
        tm = _row_tile(rows, FFN_TILE_ROWS)
        out_rows = rows
        in_spec = lambda w: pl.BlockSpec((tm, w), lambda i: (i, 0))
    else:
        lp, first, count = keep
        tm = _row_tile(count, FFN_TILE_ROWS)
        per_seq = count // tm
        out_rows = rows // lp * count
        in_spec = lambda w: pl.BlockSpec(
            (pl.Element(tm), pl.Element(w)),
            lambda i: (pl.multiple_of((i // per_seq) * lp + first + (i % per_seq) * tm, 16), 0))
    row_spec = lambda w: pl.BlockSpec((tm, w), lambda i: (i, 0))
    return pl.pallas_call(
        functools.partial(_mix_ffn_kernel, final_norm=final_norm),
        out_shape=jax.ShapeDtypeStruct((out_rows, d), F32),
        grid=(out_rows // tm,),
        in_specs=[
            in_spec(d), in_spec(attn2d.shape[1]), in_spec(c),
            _const_spec((1, c)), _const_spec((1, c)),
            _layer_spec(wo, layer), _const_spec((1, d)),
            _layer_spec(wg, layer), _layer_spec(wu, layer), _layer_spec(wd, layer),
            _const_spec((1, d)),
        ],
        out_specs=row_spec(d),
        compiler_params=pltpu.CompilerParams(
            dimension_semantics=("parallel",), vmem_limit_bytes=VMEM_LIMIT),
        name="mix_ffn",
    )(h2d, attn2d, conv2d, vec(ln_g), vec(ln_b), wo, vec(g), wg, wu, wd, vec(final_g))


def kernel(x, meta_tokens, mix_norm_g, w_in, conv_dw_w, conv_dw_b, conv_ln_g, conv_ln_b,
           w_out, ffn_norm_g, w_gate, w_up, w_down, final_norm_g):
    b, seq, d = x.shape
    n_meta = meta_tokens.shape[0]
    depth = w_in.shape[0]
    conv_ch = conv_dw_w.shape[-1]
    l = n_meta + seq
    lp = -(-l // LANES) * LANES

    meta = jnp.broadcast_to(meta_tokens.astype(x.dtype)[None], (b, n_meta, d))
    h = jnp.concatenate([meta, x, jnp.zeros((b, lp - l, d), x.dtype)], axis=1)
    h = h.reshape(b * lp, d)

    w_in_b, w_out_b, w_gate_b, w_up_b, w_down_b = (
        w.astype(BF16) for w in (w_in, w_out, w_gate, w_up, w_down))
    w_kt = jnp.swapaxes(w_in_b[:, :, SB_WIDTH:2 * SB_WIDTH], 1, 2)

    for i in range(depth):
        q, kt, v, u = _in_proj(h, mix_norm_g[i], w_in_b, w_kt, i, conv_ch)
        seq3 = lambda t: t.reshape(b, lp, t.shape[-1])
        attn = _attention(seq3(q), kt, seq3(v))
        conv = _conformer_conv(seq3(u), conv_dw_w[i], conv_dw_b[i])
        last = i == depth - 1
        keep = (lp, n_meta, seq) if last and n_meta % 16 == 0 and seq % 16 == 0 else None
        h = _mix_ffn(h, attn.reshape(b * lp, -1), conv.reshape(b * lp, -1),
                     conv_ln_g[i], conv_ln_b[i], w_out_b, ffn_norm_g[i], w_gate_b, w_up_b,
                     w_down_b, final_norm_g, i, final_norm=last, keep=keep)
    if keep is not None:
        return h.reshape(b, seq, d)
    return h.reshape(b, lp, d)[:, n_meta:l]
```

```python
import functools
import math

import jax
import jax.numpy as jnp
from jax import lax
from jax.experimental import pallas as pl
from jax.experimental.pallas import tpu as pltpu

F32 = jnp.float32
BF16 = jnp.bfloat16

EPS = 1e-6
N_HEADS = 8
HEAD_DIM = 64
SB_WIDTH = N_HEADS * HEAD_DIM
LANES = 128
HEADS_PER_BLOCK = LANES // HEAD_DIM
EXP_ZERO_CUT = -104.0
LOG_ZERO = -1e30
BAND = 3
TILES = BAND * HEADS_PER_BLOCK
FAR_ROWS = 64
STAGED_ROWS = HEADS_PER_BLOCK * (FAR_ROWS + (BAND - 1) * LANES)
VMEM_LIMIT = 56 * 1024 * 1024


def _row_tile(rows, target):
    best = 8
    for t in range(8, min(rows, target) + 1, 8):
        if rows % t == 0:
            best = t
    return best


def _const_spec(shape):
    zeros = (0,) * len(shape)
    return pl.BlockSpec(shape, lambda *_: zeros, pipeline_mode=pl.Buffered(1))


def _layer_spec(stacked, layer):
    zeros = (0,) * (stacked.ndim - 1)
    return pl.BlockSpec((None,) + stacked.shape[1:], lambda *_: (layer,) + zeros,
                        pipeline_mode=pl.Buffered(1))


def _rmsnorm(x, g):
    ms = jnp.mean(x * x, axis=-1, keepdims=True)
    return x * lax.rsqrt(ms + EPS) * g


def _in_proj_kernel(h_ref, g_ref, w_ref, wkt_ref, q_ref, kt_ref, v_ref, u_ref, *, q_scale):
    hn = _rmsnorm(h_ref[...], g_ref[...]).astype(BF16)
    sb = q_ref.shape[0] * q_ref.shape[2]
    c = u_ref.shape[-1]
    proj = lambda lo, width: jnp.dot(hn, w_ref[:, lo:lo + width], preferred_element_type=F32)

    def put_pairs(ref, y):
        for p in range(ref.shape[0]):
            ref[p] = y[:, p * LANES:(p + 1) * LANES].astype(ref.dtype)

    put_pairs(q_ref, proj(0, sb) * q_scale)
    put_pairs(v_ref, proj(2 * sb, sb))
    u_ref[...] = proj(3 * sb, c) * jax.nn.sigmoid(proj(3 * sb + c, c))
    kt = lax.dot_general(wkt_ref[...], hn, (((1,), (1,)), ((), ())), preferred_element_type=F32)
    kt_ref[...] = kt.astype(kt_ref.dtype)


def _in_proj(h2d, g, w_in, w_kt, layer, conv_ch):
    rows, d = h2d.shape
    tm = _row_tile(rows, 1024)
    n_pairs = SB_WIDTH // LANES
    by_pair = (jax.ShapeDtypeStruct((n_pairs, rows, LANES), BF16),
               pl.BlockSpec((n_pairs, tm, LANES), lambda i: (0, i, 0)))
    outs = [by_pair,
            (jax.ShapeDtypeStruct((SB_WIDTH, rows), BF16),
             pl.BlockSpec((SB_WIDTH, tm), lambda i: (0, i))),
            by_pair,
            (jax.ShapeDtypeStruct((rows, conv_ch), F32),
             pl.BlockSpec((tm, conv_ch), lambda i: (i, 0)))]
    return pl.pallas_call(
        functools.partial(_in_proj_kernel, q_scale=1.0 / math.sqrt(HEAD_DIM)),
        out_shape=[o[0] for o in outs],
        grid=(rows // tm,),
        in_specs=[
            pl.BlockSpec((tm, d), lambda i: (i, 0)),
            _const_spec((1, d)),
            _layer_spec(w_in, layer),
            _layer_spec(w_kt, layer),
        ],
        out_specs=[o[1] for o in outs],
        compiler_params=pltpu.CompilerParams(
            dimension_semantics=("parallel",), vmem_limit_bytes=VMEM_LIMIT),
        name="in_proj",
    )(h2d, g.reshape(1, d), w_in, w_kt)


def _logit_terms(z, mask):
    softplus = jnp.maximum(z, 0.0) + jnp.log(1.0 + jnp.exp(-jnp.abs(z)))
    if mask is not None:
        softplus = jnp.where(mask, softplus, 0.0)
        z = jnp.where(mask, z, LOG_ZERO)
    return softplus, z


def _hi_lo(x):
    hi = x.astype(BF16)
    return hi, (x - hi.astype(F32)).astype(BF16)


def _attn_kernel(q_ref, kt_ref, v_ref, csw_ref, o_ref,
                 kst_ref, vst_ref, hilo0, hilo1, zst0, zst1, ws0, ws1, acc_ref, carry_ref):
    n_blocks = q_ref.shape[1] // LANES
    hilo_s, zst_s, ws_s = (hilo0, hilo1), (zst0, zst1), (ws0, ws1)
    lane = lax.broadcasted_iota(jnp.int32, (LANES, LANES), 1)
    row = lax.broadcasted_iota(jnp.int32, (LANES, LANES), 0)
    diag_mask = lane < row
    zero_half = jnp.zeros((HEAD_DIM, LANES), BF16)
    zero_blk = jnp.zeros((LANES, LANES), BF16)

    def rows(i, n=LANES):
        if isinstance(i, int):
            return pl.ds(i * n, n)
        return pl.ds(pl.multiple_of(i * n, n), n)

    for j in range(BAND - 1):
        kst_ref[j] = jnp.zeros(kst_ref.shape[1:], BF16)
        vst_ref[rows(j, 2 * LANES), :] = jnp.zeros((2 * LANES, LANES), BF16)
    for j in range(n_blocks):
        kt = kt_ref[:, j * LANES:(j + 1) * LANES]
        top = jnp.concatenate([kt[:HEAD_DIM], zero_half], axis=1)
        bottom = jnp.concatenate([zero_half, kt[HEAD_DIM:]], axis=1)
        kst_ref[j + BAND - 1] = jnp.concatenate([top, bottom], axis=0)

    def stack_v(j, _):
        v = v_ref[0, rows(j), :]
        base = pl.multiple_of((j + BAND - 1) * 2 * LANES, 2 * LANES)
        vst_ref[pl.ds(base, LANES), :] = jnp.where(lane < HEAD_DIM, v, zero_blk)
        vst_ref[pl.ds(base + LANES, LANES), :] = jnp.where(lane < HEAD_DIM, zero_blk, v)
        return 0

    lax.fori_loop(0, n_blocks, stack_v, 0)

    def tile_rows(p, h):
        if p == 0:
            return slice(h * FAR_ROWS, (h + 1) * FAR_ROWS)
        start = HEADS_PER_BLOCK * FAR_ROWS + (HEADS_PER_BLOCK * (p - 1) + h) * LANES
        return slice(start, start + LANES)

    for ws in ws_s:
        ws[...] = jnp.zeros(ws.shape, BF16)

    def stage_logits(i, s, first_valid=0):
        q2 = q_ref[0, rows(i), :]
        for p in range(BAND):
            n_rows = FAR_ROWS if p == 0 else LANES
            if p < first_valid:
                for h in range(HEADS_PER_BLOCK):
                    hilo_s[s][tile_rows(p, h), :] = jnp.zeros((n_rows, 2 * LANES), BF16)
                    zst_s[s][tile_rows(p, h), :] = jnp.full((n_rows, LANES), LOG_ZERO, F32)
                continue
            z2 = jnp.dot(q2[:n_rows], kst_ref[i + p], preferred_element_type=F32)
            for h in range(HEADS_PER_BLOCK):
                softplus, z = _logit_terms(z2[:, h * LANES:(h + 1) * LANES],
                                           diag_mask if p == BAND - 1 else None)
                hi, lo = _hi_lo(softplus)
                hilo_s[s][tile_rows(p, h), 0:LANES] = hi
                hilo_s[s][tile_rows(p, h), LANES:] = lo
                zst_s[s][tile_rows(p, h), :] = z

    def stage_weights(s):
        sums = jnp.dot(hilo_s[s][...], csw_ref[...], preferred_element_type=F32)
        left = None
        for h in range(HEADS_PER_BLOCK):
            carry = jnp.zeros((LANES, LANES), F32)
            for p in reversed(range(1, BAND)):
                t = HEADS_PER_BLOCK * p + h
                blk = tile_rows(p, h)
                e = zst_s[s][blk, :] + sums[blk, :LANES] + carry
                ws_s[s][:, t * LANES:(t + 1) * LANES] = jnp.exp(e).astype(BF16)
                carry = carry + sums[blk, LANES:]
            blk = tile_rows(0, h)
            e = zst_s[s][blk, :] + sums[blk, :LANES] + carry[:FAR_ROWS]
            ws_s[s][0:FAR_ROWS, h * LANES:(h + 1) * LANES] = jnp.exp(e).astype(BF16)
            carry = jnp.concatenate([carry[:FAR_ROWS] + sums[blk, LANES:], carry[FAR_ROWS:]],
                                    axis=0)
            left = carry if left is None else jnp.maximum(left, carry)
        return left

    def unseen_keys(i):
        return jnp.logical_or(i >= BAND, jnp.logical_and(i >= BAND - 1, row >= FAR_ROWS))

    def stage_output(i, s):
        start = i * 2 * LANES
        if not isinstance(i, int):
            start = pl.multiple_of(start, 2 * LANES)
        v_band = vst_ref[pl.ds(start, BAND * 2 * LANES), :]
        out = jnp.dot(ws_s[s][...], v_band, preferred_element_type=F32)
        o_ref[0, rows(i), :] = out.astype(o_ref.dtype)

    def step(i, s, worst):
        stage_logits(i, s)
        w = stage_weights(1 - s)
        stage_output(i - 2, s)
        return jnp.where(unseen_keys(i - 1), jnp.maximum(worst, w), worst)

    worst = jnp.full((LANES, LANES), -jnp.inf, F32)
    n_pro = min(BAND - 1, n_blocks)
    for i in range(n_pro):
        stage_logits(i, i % 2, first_valid=BAND - 1 - i)
        if i >= 1:
            stage_weights((i - 1) % 2)
        if i >= 2:
            stage_output(i - 2, i % 2)
    n_main = n_blocks - n_pro
    if n_main > 0:
        assert n_pro % 2 == 0

        def pair(t, worst):
            i = n_pro + 2 * t
            worst = step(i, 0, worst)
            return step(i + 1, 1, worst)

        worst = lax.fori_loop(0, n_main // 2, pair, worst)
        if n_main % 2:
            worst = step(n_blocks - 1, (n_blocks - 1) % 2, worst)
    last = n_blocks - 1
    w = stage_weights(last % 2)
    worst = jnp.where(unseen_keys(last), jnp.maximum(worst, w), worst)
    if last >= 1:
        stage_output(last - 1, (last - 1) % 2)
    stage_output(last, last % 2)

    @pl.when(jnp.max(worst) > EXP_ZERO_CUT)
    def _():
        def q_block(i, _):
            q2 = q_ref[0, rows(i), :]
            q_pos = i * LANES + row
            acc_ref[...] = jnp.zeros_like(acc_ref)
            carry_ref[...] = jnp.zeros_like(carry_ref)

            def cond(state):
                j, alive = state
                return jnp.logical_and(j >= 0, alive > 0)

            def body(state):
                j, _ = state
                mask = (j * LANES + lane) < q_pos
                z2 = jnp.dot(q2, kst_ref[j + BAND - 1], preferred_element_type=F32)
                ws = []
                alive = None
                for h in range(HEADS_PER_BLOCK):
                    softplus, z = _logit_terms(z2[:, h * LANES:(h + 1) * LANES], mask)
                    sums = jnp.dot(jnp.concatenate(_hi_lo(softplus), axis=1), csw_ref[...],
                                   preferred_element_type=F32)
                    carry = carry_ref[h]
                    ws.append(jnp.exp(z + sums[:, :LANES] + carry).astype(BF16))
                    carry = carry + sums[:, LANES:]
                    carry_ref[h] = carry
                    alive = carry if alive is None else jnp.maximum(alive, carry)
                base = pl.multiple_of((j + BAND - 1) * 2 * LANES, 2 * LANES)
                acc_ref[...] += jnp.dot(jnp.concatenate(ws, axis=1),
                                        vst_ref[pl.ds(base, 2 * LANES), :],
                                        preferred_element_type=F32)
                return j - 1, (jnp.max(alive) > EXP_ZERO_CUT).astype(jnp.int32)

            lax.while_loop(cond, body, (i, jnp.int32(1)))
            o_ref[0, rows(i), :] = acc_ref[...].astype(o_ref.dtype)
            return 0

        lax.fori_loop(0, n_blocks, q_block, 0)


def _cumsum_weights():
    j = jnp.arange(LANES)[:, None]
    s = jnp.arange(LANES)[None, :]
    half = -jnp.concatenate([(j >= s).astype(BF16), jnp.ones((LANES, LANES), BF16)], axis=1)
    return jnp.concatenate([half, half], axis=0)


def _attention(q, kt, v, lp):
    n_pairs, rows, _ = q.shape
    b = rows // lp
    n_blocks = lp // LANES
    spec = pl.BlockSpec((1, lp, LANES), lambda bi, hi: (hi, bi, 0))
    stage = lambda shape, dt: [pltpu.VMEM(shape, dt), pltpu.VMEM(shape, dt)]
    return pl.pallas_call(
        _attn_kernel,
        out_shape=jax.ShapeDtypeStruct(q.shape, BF16),
        grid=(b, n_pairs),
        in_specs=[spec, pl.BlockSpec((LANES, lp), lambda bi, hi: (hi, bi)), spec,
                  _const_spec((2 * LANES, 2 * LANES))],
        out_specs=spec,
        scratch_shapes=[
            pltpu.VMEM((n_blocks + BAND - 1, LANES, 2 * LANES), BF16),
            pltpu.VMEM(((n_blocks + BAND - 1) * 2 * LANES, LANES), BF16),
            *stage((STAGED_ROWS, 2 * LANES), BF16),
            *stage((STAGED_ROWS, LANES), F32),
            *stage((LANES, TILES * LANES), BF16),
            pltpu.VMEM((LANES, LANES), F32),
            pltpu.VMEM((HEADS_PER_BLOCK, LANES, LANES), F32),
        ],
        compiler_params=pltpu.CompilerParams(
            dimension_semantics=("parallel", "parallel"), vmem_limit_bytes=VMEM_LIMIT),
        name="sb_attention",
    )(q, kt, v, _cumsum_weights())


CONV_HALO = 32
CONV_CHUNK = 64
CONV_TILE_ROWS = 1056


def _conv_kernel(u_ref, tail_ref, w_ref, b_ref, o_ref, *, n_taps):
    tl = u_ref.shape[1]
    w = w_ref[...]
    bias = b_ref[...]
    history = jnp.where(pl.program_id(1) > 0, tail_ref[0], 0.0)

    def convolve(window):
        acc = jnp.broadcast_to(bias, (CONV_CHUNK, bias.shape[1]))
        for r in range(8):
            shifted = window if r == 0 else pltpu.roll(window, r, 0)
            for a8 in range(0, n_taps, 8):
                s = a8 + r
                if s >= n_taps:
                    continue
                tap = n_taps - 1 - s
                lo = CONV_HALO - a8
                acc = acc + w[tap:tap + 1, :] * shifted[lo:lo + CONV_CHUNK, :]
        return acc

    o_ref[0, 0:CONV_CHUNK, :] = convolve(
        jnp.concatenate([history, u_ref[0, 0:CONV_CHUNK, :]], axis=0))

    def chunk(c, _):
        base = pl.multiple_of(c * CONV_CHUNK, 8)
        window = u_ref[0, pl.ds(base - CONV_HALO, CONV_CHUNK + CONV_HALO), :]
        o_ref[0, pl.ds(base, CONV_CHUNK), :] = convolve(window)
        return 0

    lax.fori_loop(1, tl // CONV_CHUNK, chunk, 0)


def _conformer_conv(u, dw_w, dw_b):
    b, lp, c = u.shape
    n_taps = dw_w.shape[0]
    assert n_taps - 1 <= CONV_HALO and lp % CONV_CHUNK == 0
    tl = CONV_CHUNK * max(n for n in range(1, lp // CONV_CHUNK + 1)
                          if (lp // CONV_CHUNK) % n == 0 and n * CONV_CHUNK <= CONV_TILE_ROWS)
    per_tile = tl // CONV_HALO
    tile = pl.BlockSpec((1, tl, c), lambda bi, ti: (bi, ti, 0))
    tail = pl.BlockSpec((1, CONV_HALO, c),
                        lambda bi, ti: (bi, jnp.maximum(ti * per_tile - 1, 0), 0))
    return pl.pallas_call(
        functools.partial(_conv_kernel, n_taps=n_taps),
        out_shape=jax.ShapeDtypeStruct((b, lp, c), F32),
        grid=(b, lp // tl),
        in_specs=[tile, tail, _const_spec((n_taps, c)), _const_spec((1, c))],
        out_specs=tile,
        compiler_params=pltpu.CompilerParams(
            dimension_semantics=("parallel", "parallel"), vmem_limit_bytes=VMEM_LIMIT),
        name="conformer_conv",
    )(u, u, dw_w, dw_b.reshape(1, c))


FFN_TILE_ROWS = 512


def _mix_ffn_kernel(h_ref, attn_ref, conv_ref, lng_ref, lnb_ref, wo_ref, g_ref, wg_ref, wu_ref,
                    wd_ref, fg_ref, o_ref, *, final_norm):
    attn = jnp.concatenate([attn_ref[p] for p in range(attn_ref.shape[0])], axis=1)
    sb = attn.shape[-1]
    x = conv_ref[...]
    cen = x - jnp.mean(x, axis=-1, keepdims=True)
    var = jnp.mean(cen * cen, axis=-1, keepdims=True)
    y = cen * lax.rsqrt(var + EPS) * lng_ref[...] + lnb_ref[...]
    conv = (y * jax.nn.sigmoid(y)).astype(BF16)
    h = h_ref[...]
    h = h + jnp.dot(attn, wo_ref[0:sb, :], preferred_element_type=F32)
    h = h + jnp.dot(conv, wo_ref[sb:, :], preferred_element_type=F32)
    hn = _rmsnorm(h, g_ref[...]).astype(BF16)
    gate = jnp.dot(hn, wg_ref[...], preferred_element_type=F32)
    up = jnp.dot(hn, wu_ref[...], preferred_element_type=F32)
    act = (gate * jax.nn.sigmoid(gate) * up).astype(BF16)
    h = h + jnp.dot(act, wd_ref[...], preferred_element_type=F32)
    if final_norm:
        h = _rmsnorm(h, fg_ref[...])
    o_ref[...] = h


def _mix_ffn(h2d, attn, conv2d, ln_g, ln_b, wo, g, wg, wu, wd, final_g, layer, final_norm,
             keep=None):
    rows, d = h2d.shape
    c = conv2d.shape[1]
    n_pairs = attn.shape[0]
    vec = lambda v: v.reshape(1, -1)
    if keep is None:
        tm = _row_tile(rows, FFN_TILE_ROWS)
        out_rows = rows
        in_spec = lambda w: pl.BlockSpec((tm, w), lambda i: (i, 0))
        attn_spec = pl.BlockSpec((n_pairs, tm, LANES), lambda i: (0, i, 0))
    else:
        lp, first, count = keep
        tm = _row_tile(count, FFN_TILE_ROWS)
        per_seq = count // tm
        out_rows = rows // lp * count
        start = lambda i: pl.multiple_of((i // per_seq) * lp + first + (i % per_seq) * tm, 16)
        in_spec = lambda w: pl.BlockSpec((pl.Element(tm), pl.Element(w)),
                                         lambda i: (start(i), 0))
        attn_spec = pl.BlockSpec((pl.Element(n_pairs), pl.Element(tm), pl.Element(LANES)),
                                 lambda i: (0, start(i), 0))
    row_spec = lambda w: pl.BlockSpec((tm, w), lambda i: (i, 0))
    return pl.pallas_call(
        functools.partial(_mix_ffn_kernel, final_norm=final_norm),
        out_shape=jax.ShapeDtypeStruct((out_rows, d), F32),
        grid=(out_rows // tm,),
        in_specs=[
            in_spec(d), attn_spec, in_spec(c),
            _const_spec((1, c)), _const_spec((1, c)),
            _layer_spec(wo, layer), _const_spec((1, d)),
            _layer_spec(wg, layer), _layer_spec(wu, layer), _layer_spec(wd, layer),
            _const_spec((1, d)),
        ],
        out_specs=row_spec(d),
        compiler_params=pltpu.CompilerParams(
            dimension_semantics=("parallel",), vmem_limit_bytes=VMEM_LIMIT),
        name="mix_ffn",
    )(h2d, attn, conv2d, vec(ln_g), vec(ln_b), wo, vec(g), wg, wu, wd, vec(final_g))


def kernel(x, meta_tokens, mix_norm_g, w_in, conv_dw_w, conv_dw_b, conv_ln_g, conv_ln_b,
           w_out, ffn_norm_g, w_gate, w_up, w_down, final_norm_g):
    b, seq, d = x.shape
    n_meta = meta_tokens.shape[0]
    depth = w_in.shape[0]
    conv_ch = conv_dw_w.shape[-1]
    l = n_meta + seq
    lp = -(-l // LANES) * LANES

    meta = jnp.broadcast_to(meta_tokens.astype(x.dtype)[None], (b, n_meta, d))
    h = jnp.concatenate([meta, x, jnp.zeros((b, lp - l, d), x.dtype)], axis=1)
    h = h.reshape(b * lp, d)

    w_in_b, w_out_b, w_gate_b, w_up_b, w_down_b = (
        w.astype(BF16) for w in (w_in, w_out, w_gate, w_up, w_down))
    w_kt = jnp.swapaxes(w_in_b[:, :, SB_WIDTH:2 * SB_WIDTH], 1, 2)

    for i in range(depth):
        q, kt, v, u = _in_proj(h, mix_norm_g[i], w_in_b, w_kt, i, conv_ch)
        attn = _attention(q, kt, v, lp)
        conv = _conformer_conv(u.reshape(b, lp, -1), conv_dw_w[i], conv_dw_b[i])
        last = i == depth - 1
        keep = (lp, n_meta, seq) if last and n_meta % 16 == 0 and seq % 16 == 0 else None
        h = _mix_ffn(h, attn, conv.reshape(b * lp, -1),
                     conv_ln_g[i], conv_ln_b[i], w_out_b, ffn_norm_g[i], w_gate_b, w_up_b,
                     w_down_b, final_norm_g, i, final_norm=last, keep=keep)
    if keep is not None:
        return h.reshape(b, seq, d)
    return h.reshape(b, lp, d)[:, n_meta:l]
```

```python
import functools
import math

import jax
import jax.numpy as jnp
from jax import lax
from jax.experimental import pallas as pl
from jax.experimental.pallas import tpu as pltpu

F32 = jnp.float32
BF16 = jnp.bfloat16

EPS = 1e-6
N_HEADS = 8
HEAD_DIM = 64
SB_WIDTH = N_HEADS * HEAD_DIM
LANES = 128
HEADS_PER_BLOCK = LANES // HEAD_DIM
EXP_ZERO_CUT = -104.0
LOG_ZERO = -1e30
BAND = 3
TILES = BAND * HEADS_PER_BLOCK
FAR_ROWS = 64
STAGED_ROWS = HEADS_PER_BLOCK * (FAR_ROWS + (BAND - 1) * LANES)
VMEM_LIMIT = 56 * 1024 * 1024


def _row_tile(rows, target):
    best = 8
    for t in range(8, min(rows, target) + 1, 8):
        if rows % t == 0:
            best = t
    return best


def _const_spec(shape):
    zeros = (0,) * len(shape)
    return pl.BlockSpec(shape, lambda *_: zeros, pipeline_mode=pl.Buffered(1))


def _layer_spec(stacked, layer):
    zeros = (0,) * (stacked.ndim - 1)
    return pl.BlockSpec((None,) + stacked.shape[1:], lambda *_: (layer,) + zeros,
                        pipeline_mode=pl.Buffered(1))


def _rmsnorm(x, g):
    ms = jnp.mean(x * x, axis=-1, keepdims=True)
    return x * lax.rsqrt(ms + EPS) * g


def _in_proj_kernel(h_ref, g_ref, w_ref, wkt_ref, q_ref, kt_ref, v_ref, u_ref, *, q_scale):
    _project(h_ref[...], g_ref, w_ref, wkt_ref, q_ref, kt_ref, v_ref, u_ref, q_scale)


def _first_in_proj_kernel(x_ref, meta_ref, g_ref, w_ref, wkt_ref, h_ref, q_ref, kt_ref, v_ref,
                          u_ref, *, q_scale, tiles_per_seq, pad):
    i = pl.program_id(0)
    n_meta = meta_ref.shape[0]
    xt = x_ref[...]
    tm = xt.shape[0]
    late = jnp.concatenate([xt[:n_meta], xt[:tm - n_meta]], axis=0)
    early = jnp.concatenate([xt[pad:], xt[:pad]], axis=0)
    body = jnp.where(i == 0, late, jnp.where(i == pl.num_programs(0) - 1, early, xt))
    top = jnp.where(i % tiles_per_seq == 0, meta_ref[...], body[:n_meta])
    h = jnp.concatenate([top, body[n_meta:]], axis=0)
    h_ref[...] = h
    _project(h, g_ref, w_ref, wkt_ref, q_ref, kt_ref, v_ref, u_ref, q_scale)


def _project(h, g_ref, w_ref, wkt_ref, q_ref, kt_ref, v_ref, u_ref, q_scale):
    hn = _rmsnorm(h, g_ref[...]).astype(BF16)
    sb = q_ref.shape[0] * q_ref.shape[2]
    c = u_ref.shape[-1]
    proj = lambda lo, width: jnp.dot(hn, w_ref[:, lo:lo + width], preferred_element_type=F32)

    def put_pairs(ref, y):
        for p in range(ref.shape[0]):
            ref[p] = y[:, p * LANES:(p + 1) * LANES].astype(ref.dtype)

    put_pairs(q_ref, proj(0, sb) * q_scale)
    put_pairs(v_ref, proj(2 * sb, sb))
    u_ref[...] = proj(3 * sb, c) * jax.nn.sigmoid(proj(3 * sb + c, c))
    kt = lax.dot_general(wkt_ref[...], hn, (((1,), (1,)), ((), ())), preferred_element_type=F32)
    kt_ref[...] = kt.astype(kt_ref.dtype)


def _in_proj(h2d, g, w_in, w_kt, layer, conv_ch):
    rows, d = h2d.shape
    tm = _row_tile(rows, 1024)
    return pl.pallas_call(
        functools.partial(_in_proj_kernel, q_scale=1.0 / math.sqrt(HEAD_DIM)),
        grid=(rows // tm,),
        in_specs=[
            pl.BlockSpec((tm, d), lambda i: (i, 0)),
            _const_spec((1, d)),
            _layer_spec(w_in, layer),
            _layer_spec(w_kt, layer),
        ],
        compiler_params=pltpu.CompilerParams(
            dimension_semantics=("parallel",), vmem_limit_bytes=VMEM_LIMIT),
        name="in_proj",
        **_proj_outputs(rows, tm, conv_ch),
    )(h2d, g.reshape(1, d), w_in, w_kt)


def _proj_outputs(rows, tm, conv_ch, extra=()):
    n_pairs = SB_WIDTH // LANES
    by_pair = (jax.ShapeDtypeStruct((n_pairs, rows, LANES), BF16),
               pl.BlockSpec((n_pairs, tm, LANES), lambda i: (0, i, 0)))
    outs = list(extra) + [
        by_pair,
        (jax.ShapeDtypeStruct((SB_WIDTH, rows), BF16),
         pl.BlockSpec((SB_WIDTH, tm), lambda i: (0, i))),
        by_pair,
        (jax.ShapeDtypeStruct((rows, conv_ch), F32),
         pl.BlockSpec((tm, conv_ch), lambda i: (i, 0)))]
    return dict(out_shape=[o[0] for o in outs], out_specs=[o[1] for o in outs])


FIRST_TILE_ROWS = 1408


def _first_tile(b, seq, n_meta, lp):
    tiles = [t for t in range(LANES, min(lp, FIRST_TILE_ROWS) + 1, LANES) if lp % t == 0]
    tm = max(tiles) if tiles else 0
    ok = (tm and lp // tm >= 2 and b * seq >= tm and n_meta % 8 == 0
          and (lp - n_meta - seq) % 8 == 0)
    return tm if ok else None


def _first_in_proj(x, meta, lp, tm, g, w_in, w_kt, conv_ch):
    b, seq, d = x.shape
    n_meta = meta.shape[0]
    rows = b * lp
    per_seq = lp // tm
    pad = lp - n_meta - seq

    def start(i):
        wanted = (i // per_seq) * seq + (i % per_seq) * tm - n_meta
        return pl.multiple_of(jnp.clip(wanted, 0, b * seq - tm), 8)

    h_out = (jax.ShapeDtypeStruct((rows, d), F32), pl.BlockSpec((tm, d), lambda i: (i, 0)))
    return pl.pallas_call(
        functools.partial(_first_in_proj_kernel, q_scale=1.0 / math.sqrt(HEAD_DIM),
                          tiles_per_seq=per_seq, pad=pad),
        grid=(rows // tm,),
        in_specs=[
            pl.BlockSpec((pl.Element(tm), pl.Element(d)), lambda i: (start(i), 0)),
            _const_spec(meta.shape),
            _const_spec((1, d)),
            _layer_spec(w_in, 0),
            _layer_spec(w_kt, 0),
        ],
        compiler_params=pltpu.CompilerParams(
            dimension_semantics=("parallel",), vmem_limit_bytes=VMEM_LIMIT),
        name="first_in_proj",
        **_proj_outputs(rows, tm, conv_ch, extra=[h_out]),
    )(x.reshape(b * seq, d), meta, g.reshape(1, d), w_in, w_kt)


def _logit_terms(z, mask):
    softplus = jnp.maximum(z, 0.0) + jnp.log(1.0 + jnp.exp(-jnp.abs(z)))
    if mask is not None:
        softplus = jnp.where(mask, softplus, 0.0)
        z = jnp.where(mask, z, LOG_ZERO)
    return softplus, z


def _hi_lo(x):
    hi = x.astype(BF16)
    return hi, (x - hi.astype(F32)).astype(BF16)


def _attn_kernel(q_ref, kt_ref, v_ref, csw_ref, o_ref,
                 kst_ref, vst_ref, hilo0, hilo1, zst0, zst1, ws0, ws1, acc_ref, carry_ref):
    n_blocks = q_ref.shape[1] // LANES
    hilo_s, zst_s, ws_s = (hilo0, hilo1), (zst0, zst1), (ws0, ws1)
    lane = lax.broadcasted_iota(jnp.int32, (LANES, LANES), 1)
    row = lax.broadcasted_iota(jnp.int32, (LANES, LANES), 0)
    diag_mask = lane < row
    zero_half = jnp.zeros((HEAD_DIM, LANES), BF16)
    zero_blk = jnp.zeros((LANES, LANES), BF16)

    def rows(i, n=LANES):
        if isinstance(i, int):
            return pl.ds(i * n, n)
        return pl.ds(pl.multiple_of(i * n, n), n)

    for j in range(BAND - 1):
        kst_ref[j] = jnp.zeros(kst_ref.shape[1:], BF16)
        vst_ref[rows(j, 2 * LANES), :] = jnp.zeros((2 * LANES, LANES), BF16)
    for j in range(n_blocks):
        kt = kt_ref[:, j * LANES:(j + 1) * LANES]
        top = jnp.concatenate([kt[:HEAD_DIM], zero_half], axis=1)
        bottom = jnp.concatenate([zero_half, kt[HEAD_DIM:]], axis=1)
        kst_ref[j + BAND - 1] = jnp.concatenate([top, bottom], axis=0)

    def stack_v(j, _):
        v = v_ref[0, rows(j), :]
        base = pl.multiple_of((j + BAND - 1) * 2 * LANES, 2 * LANES)
        vst_ref[pl.ds(base, LANES), :] = jnp.where(lane < HEAD_DIM, v, zero_blk)
        vst_ref[pl.ds(base + LANES, LANES), :] = jnp.where(lane < HEAD_DIM, zero_blk, v)
        return 0

    lax.fori_loop(0, n_blocks, stack_v, 0)

    def tile_rows(p, h):
        if p == 0:
            return slice(h * FAR_ROWS, (h + 1) * FAR_ROWS)
        start = HEADS_PER_BLOCK * FAR_ROWS + (HEADS_PER_BLOCK * (p - 1) + h) * LANES
        return slice(start, start + LANES)

    for ws in ws_s:
        ws[...] = jnp.zeros(ws.shape, BF16)

    def stage_logits(i, s, first_valid=0):
        q2 = q_ref[0, rows(i), :]
        for p in range(BAND):
            n_rows = FAR_ROWS if p == 0 else LANES
            if p < first_valid:
                for h in range(HEADS_PER_BLOCK):
                    hilo_s[s][tile_rows(p, h), :] = jnp.zeros((n_rows, 2 * LANES), BF16)
                    zst_s[s][tile_rows(p, h), :] = jnp.full((n_rows, LANES), LOG_ZERO, F32)
                continue
            z2 = jnp.dot(q2[:n_rows], kst_ref[i + p], preferred_element_type=F32)
            for h in range(HEADS_PER_BLOCK):
                softplus, z = _logit_terms(z2[:, h * LANES:(h + 1) * LANES],
                                           diag_mask if p == BAND - 1 else None)
                hi, lo = _hi_lo(softplus)
                hilo_s[s][tile_rows(p, h), 0:LANES] = hi
                hilo_s[s][tile_rows(p, h), LANES:] = lo
                zst_s[s][tile_rows(p, h), :] = z

    def stage_weights(s):
        sums = jnp.dot(hilo_s[s][...], csw_ref[...], preferred_element_type=F32)
        left = None
        for h in range(HEADS_PER_BLOCK):
            carry = jnp.zeros((LANES, LANES), F32)
            for p in reversed(range(1, BAND)):
                t = HEADS_PER_BLOCK * p + h
                blk = tile_rows(p, h)
                e = zst_s[s][blk, :] + sums[blk, :LANES] + carry
                ws_s[s][:, t * LANES:(t + 1) * LANES] = jnp.exp(e).astype(BF16)
                carry = carry + sums[blk, LANES:]
            blk = tile_rows(0, h)
            e = zst_s[s][blk, :] + sums[blk, :LANES] + carry[:FAR_ROWS]
            ws_s[s][0:FAR_ROWS, h * LANES:(h + 1) * LANES] = jnp.exp(e).astype(BF16)
            carry = jnp.concatenate([carry[:FAR_ROWS] + sums[blk, LANES:], carry[FAR_ROWS:]],
                                    axis=0)
            left = carry if left is None else jnp.maximum(left, carry)
        return left

    def unseen_keys(i):
        return jnp.logical_or(i >= BAND, jnp.logical_and(i >= BAND - 1, row >= FAR_ROWS))

    def stage_output(i, s):
        start = i * 2 * LANES
        if not isinstance(i, int):
            start = pl.multiple_of(start, 2 * LANES)
        v_band = vst_ref[pl.ds(start, BAND * 2 * LANES), :]
        out = jnp.dot(ws_s[s][...], v_band, preferred_element_type=F32)
        o_ref[0, rows(i), :] = out.astype(o_ref.dtype)

    def step(i, s, worst):
        stage_logits(i, s)
        w = stage_weights(1 - s)
        stage_output(i - 2, s)
        return jnp.where(unseen_keys(i - 1), jnp.maximum(worst, w), worst)

    worst = jnp.full((LANES, LANES), -jnp.inf, F32)
    n_pro = min(BAND - 1, n_blocks)
    for i in range(n_pro):
        stage_logits(i, i % 2, first_valid=BAND - 1 - i)
        if i >= 1:
            stage_weights((i - 1) % 2)
        if i >= 2:
            stage_output(i - 2, i % 2)
    n_main = n_blocks - n_pro
    if n_main > 0:
        assert n_pro % 2 == 0

        def pair(t, worst):
            i = n_pro + 2 * t
            worst = step(i, 0, worst)
            return step(i + 1, 1, worst)

        worst = lax.fori_loop(0, n_main // 2, pair, worst)
        if n_main % 2:
            worst = step(n_blocks - 1, (n_blocks - 1) % 2, worst)
    last = n_blocks - 1
    w = stage_weights(last % 2)
    worst = jnp.where(unseen_keys(last), jnp.maximum(worst, w), worst)
    if last >= 1:
        stage_output(last - 1, (last - 1) % 2)
    stage_output(last, last % 2)

    @pl.when(jnp.max(worst) > EXP_ZERO_CUT)
    def _():
        def q_block(i, _):
            q2 = q_ref[0, rows(i), :]
            q_pos = i * LANES + row
            acc_ref[...] = jnp.zeros_like(acc_ref)
            carry_ref[...] = jnp.zeros_like(carry_ref)

            def cond(state):
                j, alive = state
                return jnp.logical_and(j >= 0, alive > 0)

            def body(state):
                j, _ = state
                mask = (j * LANES + lane) < q_pos
                z2 = jnp.dot(q2, kst_ref[j + BAND - 1], preferred_element_type=F32)
                ws = []
                alive = None
                for h in range(HEADS_PER_BLOCK):
                    softplus, z = _logit_terms(z2[:, h * LANES:(h + 1) * LANES], mask)
                    sums = jnp.dot(jnp.concatenate(_hi_lo(softplus), axis=1), csw_ref[...],
                                   preferred_element_type=F32)
                    carry = carry_ref[h]
                    ws.append(jnp.exp(z + sums[:, :LANES] + carry).astype(BF16))
                    carry = carry + sums[:, LANES:]
                    carry_ref[h] = carry
                    alive = carry if alive is None else jnp.maximum(alive, carry)
                base = pl.multiple_of((j + BAND - 1) * 2 * LANES, 2 * LANES)
                acc_ref[...] += jnp.dot(jnp.concatenate(ws, axis=1),
                                        vst_ref[pl.ds(base, 2 * LANES), :],
                                        preferred_element_type=F32)
                return j - 1, (jnp.max(alive) > EXP_ZERO_CUT).astype(jnp.int32)

            lax.while_loop(cond, body, (i, jnp.int32(1)))
            o_ref[0, rows(i), :] = acc_ref[...].astype(o_ref.dtype)
            return 0

        lax.fori_loop(0, n_blocks, q_block, 0)


def _cumsum_weights():
    j = jnp.arange(LANES)[:, None]
    s = jnp.arange(LANES)[None, :]
    half = -jnp.concatenate([(j >= s).astype(BF16), jnp.ones((LANES, LANES), BF16)], axis=1)
    return jnp.concatenate([half, half], axis=0)


def _attention(q, kt, v, lp):
    n_pairs, rows, _ = q.shape
    b = rows // lp
    n_blocks = lp // LANES
    spec = pl.BlockSpec((1, lp, LANES), lambda bi, hi: (hi, bi, 0))
    stage = lambda shape, dt: [pltpu.VMEM(shape, dt), pltpu.VMEM(shape, dt)]
    return pl.pallas_call(
        _attn_kernel,
        out_shape=jax.ShapeDtypeStruct(q.shape, BF16),
        grid=(b, n_pairs),
        in_specs=[spec, pl.BlockSpec((LANES, lp), lambda bi, hi: (hi, bi)), spec,
                  _const_spec((2 * LANES, 2 * LANES))],
        out_specs=spec,
        scratch_shapes=[
            pltpu.VMEM((n_blocks + BAND - 1, LANES, 2 * LANES), BF16),
            pltpu.VMEM(((n_blocks + BAND - 1) * 2 * LANES, LANES), BF16),
            *stage((STAGED_ROWS, 2 * LANES), BF16),
            *stage((STAGED_ROWS, LANES), F32),
            *stage((LANES, TILES * LANES), BF16),
            pltpu.VMEM((LANES, LANES), F32),
            pltpu.VMEM((HEADS_PER_BLOCK, LANES, LANES), F32),
        ],
        compiler_params=pltpu.CompilerParams(
            dimension_semantics=("parallel", "parallel"), vmem_limit_bytes=VMEM_LIMIT),
        name="sb_attention",
    )(q, kt, v, _cumsum_weights())


CONV_HALO = 32
CONV_CHUNK = 64
CONV_TILE_ROWS = 1056


def _conv_kernel(u_ref, tail_ref, w_ref, b_ref, o_ref, *, n_taps):
    tl = u_ref.shape[1]
    w = w_ref[...]
    bias = b_ref[...]
    history = jnp.where(pl.program_id(1) > 0, tail_ref[0], 0.0)

    def convolve(window):
        acc = jnp.broadcast_to(bias, (CONV_CHUNK, bias.shape[1]))
        for r in range(8):
            shifted = window if r == 0 else pltpu.roll(window, r, 0)
            for a8 in range(0, n_taps, 8):
                s = a8 + r
                if s >= n_taps:
                    continue
                tap = n_taps - 1 - s
                lo = CONV_HALO - a8
                acc = acc + w[tap:tap + 1, :] * shifted[lo:lo + CONV_CHUNK, :]
        return acc

    o_ref[0, 0:CONV_CHUNK, :] = convolve(
        jnp.concatenate([history, u_ref[0, 0:CONV_CHUNK, :]], axis=0))

    def chunk(c, _):
        base = pl.multiple_of(c * CONV_CHUNK, 8)
        window = u_ref[0, pl.ds(base - CONV_HALO, CONV_CHUNK + CONV_HALO), :]
        o_ref[0, pl.ds(base, CONV_CHUNK), :] = convolve(window)
        return 0

    lax.fori_loop(1, tl // CONV_CHUNK, chunk, 0)


def _conformer_conv(u, dw_w, dw_b):
    b, lp, c = u.shape
    n_taps = dw_w.shape[0]
    assert n_taps - 1 <= CONV_HALO and lp % CONV_CHUNK == 0
    tl = CONV_CHUNK * max(n for n in range(1, lp // CONV_CHUNK + 1)
                          if (lp // CONV_CHUNK) % n == 0 and n * CONV_CHUNK <= CONV_TILE_ROWS)
    per_tile = tl // CONV_HALO
    tile = pl.BlockSpec((1, tl, c), lambda bi, ti: (bi, ti, 0))
    tail = pl.BlockSpec((1, CONV_HALO, c),
                        lambda bi, ti: (bi, jnp.maximum(ti * per_tile - 1, 0), 0))
    return pl.pallas_call(
        functools.partial(_conv_kernel, n_taps=n_taps),
        out_shape=jax.ShapeDtypeStruct((b, lp, c), F32),
        grid=(b, lp // tl),
        in_specs=[tile, tail, _const_spec((n_taps, c)), _const_spec((1, c))],
        out_specs=tile,
        compiler_params=pltpu.CompilerParams(
            dimension_semantics=("parallel", "parallel"), vmem_limit_bytes=VMEM_LIMIT),
        name="conformer_conv",
    )(u, u, dw_w, dw_b.reshape(1, c))


FFN_TILE_ROWS = 512


def _mix_ffn_kernel(h_ref, attn_ref, conv_ref, lng_ref, lnb_ref, wo_ref, g_ref, wg_ref, wu_ref,
                    wd_ref, fg_ref, o_ref, *, final_norm):
    attn = jnp.concatenate([attn_ref[p] for p in range(attn_ref.shape[0])], axis=1)
    sb = attn.shape[-1]
    x = conv_ref[...]
    cen = x - jnp.mean(x, axis=-1, keepdims=True)
    var = jnp.mean(cen * cen, axis=-1, keepdims=True)
    y = cen * lax.rsqrt(var + EPS) * lng_ref[...] + lnb_ref[...]
    conv = (y * jax.nn.sigmoid(y)).astype(BF16)
    h = h_ref[...]
    h = h + jnp.dot(attn, wo_ref[0:sb, :], preferred_element_type=F32)
    h = h + jnp.dot(conv, wo_ref[sb:, :], preferred_element_type=F32)
    hn = _rmsnorm(h, g_ref[...]).astype(BF16)
    gate = jnp.dot(hn, wg_ref[...], preferred_element_type=F32)
    up = jnp.dot(hn, wu_ref[...], preferred_element_type=F32)
    act = (gate * jax.nn.sigmoid(gate) * up).astype(BF16)
    h = h + jnp.dot(act, wd_ref[...], preferred_element_type=F32)
    if final_norm:
        h = _rmsnorm(h, fg_ref[...])
    o_ref[...] = h


def _mix_ffn(h2d, attn, conv2d, ln_g, ln_b, wo, g, wg, wu, wd, final_g, layer, final_norm,
             keep=None):
    rows, d = h2d.shape
    c = conv2d.shape[1]
    n_pairs = attn.shape[0]
    vec = lambda v: v.reshape(1, -1)
    if keep is None:
        tm = _row_tile(rows, FFN_TILE_ROWS)
        out_rows = rows
        in_spec = lambda w: pl.BlockSpec((tm, w), lambda i: (i, 0))
        attn_spec = pl.BlockSpec((n_pairs, tm, LANES), lambda i: (0, i, 0))
    else:
        lp, first, count = keep
        tm = _row_tile(count, FFN_TILE_ROWS)
        per_seq = count // tm
        out_rows = rows // lp * count
        start = lambda i: pl.multiple_of((i // per_seq) * lp + first + (i % per_seq) * tm, 16)
        in_spec = lambda w: pl.BlockSpec((pl.Element(tm), pl.Element(w)),
                                         lambda i: (start(i), 0))
        attn_spec = pl.BlockSpec((pl.Element(n_pairs), pl.Element(tm), pl.Element(LANES)),
                                 lambda i: (0, start(i), 0))
    row_spec = lambda w: pl.BlockSpec((tm, w), lambda i: (i, 0))
    return pl.pallas_call(
        functools.partial(_mix_ffn_kernel, final_norm=final_norm),
        out_shape=jax.ShapeDtypeStruct((out_rows, d), F32),
        grid=(out_rows // tm,),
        in_specs=[
            in_spec(d), attn_spec, in_spec(c),
            _const_spec((1, c)), _const_spec((1, c)),
            _layer_spec(wo, layer), _const_spec((1, d)),
            _layer_spec(wg, layer), _layer_spec(wu, layer), _layer_spec(wd, layer),
            _const_spec((1, d)),
        ],
        out_specs=row_spec(d),
        compiler_params=pltpu.CompilerParams(
            dimension_semantics=("parallel",), vmem_limit_bytes=VMEM_LIMIT),
        name="mix_ffn",
    )(h2d, attn, conv2d, vec(ln_g), vec(ln_b), wo, vec(g), wg, wu, wd, vec(final_g))


def kernel(x, meta_tokens, mix_norm_g, w_in, conv_dw_w, conv_dw_b, conv_ln_g, conv_ln_b,
           w_out, ffn_norm_g, w_gate, w_up, w_down, final_norm_g):
    b, seq, d = x.shape
    n_meta = meta_tokens.shape[0]
    depth = w_in.shape[0]
    conv_ch = conv_dw_w.shape[-1]
    l = n_meta + seq
    lp = -(-l // LANES) * LANES

    first_tm = _first_tile(b, seq, n_meta, lp)
    if first_tm is None:
        meta = jnp.broadcast_to(meta_tokens.astype(x.dtype)[None], (b, n_meta, d))
        h = jnp.concatenate([meta, x, jnp.zeros((b, lp - l, d), x.dtype)], axis=1)
        h = h.reshape(b * lp, d)

    w_in_b, w_out_b, w_gate_b, w_up_b, w_down_b = (
        w.astype(BF16) for w in (w_in, w_out, w_gate, w_up, w_down))
    w_kt = jnp.swapaxes(w_in_b[:, :, SB_WIDTH:2 * SB_WIDTH], 1, 2)

    for i in range(depth):
        if i == 0 and first_tm is not None:
            h, q, kt, v, u = _first_in_proj(x, meta_tokens.astype(x.dtype), lp, first_tm,
                                            mix_norm_g[0], w_in_b, w_kt, conv_ch)
        else:
            q, kt, v, u = _in_proj(h, mix_norm_g[i], w_in_b, w_kt, i, conv_ch)
        attn = _attention(q, kt, v, lp)
        conv = _conformer_conv(u.reshape(b, lp, -1), conv_dw_w[i], conv_dw_b[i])
        last = i == depth - 1
        keep = (lp, n_meta, seq) if last and n_meta % 16 == 0 and seq % 16 == 0 else None
        h = _mix_ffn(h, attn, conv.reshape(b * lp, -1),
                     conv_ln_g[i], conv_ln_b[i], w_out_b, ffn_norm_g[i], w_gate_b, w_up_b,
                     w_down_b, final_norm_g, i, final_norm=last, keep=keep)
    if keep is not None:
        return h.reshape(b, seq, d)
    return h.reshape(b, lp, d)[:, n_meta:l]
```

```python
import functools
import math

import jax
import jax.numpy as jnp
from jax import lax
from jax.experimental import pallas as pl
from jax.experimental.pallas import tpu as pltpu

F32 = jnp.float32
BF16 = jnp.bfloat16

EPS = 1e-6
N_HEADS = 8
HEAD_DIM = 64
SB_WIDTH = N_HEADS * HEAD_DIM
LANES = 128
HEADS_PER_BLOCK = LANES // HEAD_DIM
EXP_ZERO_CUT = -104.0
LOG_ZERO = -1e30
BAND = 3
FAR_ROWS = 64
STAGED_ROWS = HEADS_PER_BLOCK * (FAR_ROWS + (BAND - 1) * LANES)
VMEM_LIMIT = 56 * 1024 * 1024


def _row_tile(rows, target):
    best = 8
    for t in range(8, min(rows, target) + 1, 8):
        if rows % t == 0:
            best = t
    return best


def _const_spec(shape):
    zeros = (0,) * len(shape)
    return pl.BlockSpec(shape, lambda *_: zeros, pipeline_mode=pl.Buffered(1))


def _layer_spec(stacked, layer):
    zeros = (0,) * (stacked.ndim - 1)
    return pl.BlockSpec((None,) + stacked.shape[1:], lambda *_: (layer,) + zeros,
                        pipeline_mode=pl.Buffered(1))


def _rmsnorm(x, g):
    ms = jnp.mean(x * x, axis=-1, keepdims=True)
    return x * lax.rsqrt(ms + EPS) * g


def _in_proj_kernel(h_ref, g_ref, w_ref, wkt_ref, q_ref, kt_ref, v_ref, u_ref, *, q_scale):
    _project(h_ref[...], g_ref, w_ref, wkt_ref, q_ref, kt_ref, v_ref, u_ref, q_scale)


def _first_in_proj_kernel(x_ref, meta_ref, g_ref, w_ref, wkt_ref, h_ref, q_ref, kt_ref, v_ref,
                          u_ref, *, q_scale, tiles_per_seq, pad):
    i = pl.program_id(0)
    n_meta = meta_ref.shape[0]
    xt = x_ref[...]
    tm = xt.shape[0]
    late = jnp.concatenate([xt[:n_meta], xt[:tm - n_meta]], axis=0)
    early = jnp.concatenate([xt[pad:], xt[:pad]], axis=0)
    body = jnp.where(i == 0, late, jnp.where(i == pl.num_programs(0) - 1, early, xt))
    top = jnp.where(i % tiles_per_seq == 0, meta_ref[...], body[:n_meta])
    h = jnp.concatenate([top, body[n_meta:]], axis=0)
    h_ref[...] = h
    _project(h, g_ref, w_ref, wkt_ref, q_ref, kt_ref, v_ref, u_ref, q_scale)


def _project(h, g_ref, w_ref, wkt_ref, q_ref, kt_ref, v_ref, u_ref, q_scale):
    hn = _rmsnorm(h, g_ref[...]).astype(BF16)
    sb = q_ref.shape[0] * q_ref.shape[2]
    c = u_ref.shape[-1]
    proj = lambda lo, width: jnp.dot(hn, w_ref[:, lo:lo + width], preferred_element_type=F32)

    def put_pairs(ref, y):
        for p in range(ref.shape[0]):
            ref[p] = y[:, p * LANES:(p + 1) * LANES].astype(ref.dtype)

    put_pairs(q_ref, proj(0, sb) * q_scale)
    put_pairs(v_ref, proj(2 * sb, sb))
    u_ref[...] = proj(3 * sb, c) * jax.nn.sigmoid(proj(3 * sb + c, c))
    kt = lax.dot_general(wkt_ref[...], hn, (((1,), (1,)), ((), ())), preferred_element_type=F32)
    kt_ref[...] = kt.astype(kt_ref.dtype)


def _in_proj(h2d, g, w_in, w_kt, layer, conv_ch):
    rows, d = h2d.shape
    tm = _row_tile(rows, 1024)
    return pl.pallas_call(
        functools.partial(_in_proj_kernel, q_scale=1.0 / math.sqrt(HEAD_DIM)),
        grid=(rows // tm,),
        in_specs=[
            pl.BlockSpec((tm, d), lambda i: (i, 0)),
            _const_spec((1, d)),
            _layer_spec(w_in, layer),
            _layer_spec(w_kt, layer),
        ],
        compiler_params=pltpu.CompilerParams(
            dimension_semantics=("parallel",), vmem_limit_bytes=VMEM_LIMIT),
        name="in_proj",
        **_proj_outputs(rows, tm, conv_ch),
    )(h2d, g.reshape(1, d), w_in, w_kt)


def _proj_outputs(rows, tm, conv_ch, extra=()):
    n_pairs = SB_WIDTH // LANES
    by_pair = (jax.ShapeDtypeStruct((n_pairs, rows, LANES), BF16),
               pl.BlockSpec((n_pairs, tm, LANES), lambda i: (0, i, 0)))
    outs = list(extra) + [
        by_pair,
        (jax.ShapeDtypeStruct((SB_WIDTH, rows), BF16),
         pl.BlockSpec((SB_WIDTH, tm), lambda i: (0, i))),
        by_pair,
        (jax.ShapeDtypeStruct((rows, conv_ch), F32),
         pl.BlockSpec((tm, conv_ch), lambda i: (i, 0)))]
    return dict(out_shape=[o[0] for o in outs], out_specs=[o[1] for o in outs])


FIRST_TILE_ROWS = 1408


def _first_tile(b, seq, n_meta, lp):
    tiles = [t for t in range(LANES, min(lp, FIRST_TILE_ROWS) + 1, LANES) if lp % t == 0]
    tm = max(tiles) if tiles else 0
    ok = (tm and lp // tm >= 2 and b * seq >= tm and n_meta % 8 == 0
          and (lp - n_meta - seq) % 8 == 0)
    return tm if ok else None


def _first_in_proj(x, meta, lp, tm, g, w_in, w_kt, conv_ch):
    b, seq, d = x.shape
    n_meta = meta.shape[0]
    rows = b * lp
    per_seq = lp // tm
    pad = lp - n_meta - seq

    def start(i):
        wanted = (i // per_seq) * seq + (i % per_seq) * tm - n_meta
        return pl.multiple_of(jnp.clip(wanted, 0, b * seq - tm), 8)

    h_out = (jax.ShapeDtypeStruct((rows, d), F32), pl.BlockSpec((tm, d), lambda i: (i, 0)))
    return pl.pallas_call(
        functools.partial(_first_in_proj_kernel, q_scale=1.0 / math.sqrt(HEAD_DIM),
                          tiles_per_seq=per_seq, pad=pad),
        grid=(rows // tm,),
        in_specs=[
            pl.BlockSpec((pl.Element(tm), pl.Element(d)), lambda i: (start(i), 0)),
            _const_spec(meta.shape),
            _const_spec((1, d)),
            _layer_spec(w_in, 0),
            _layer_spec(w_kt, 0),
        ],
        compiler_params=pltpu.CompilerParams(
            dimension_semantics=("parallel",), vmem_limit_bytes=VMEM_LIMIT),
        name="first_in_proj",
        **_proj_outputs(rows, tm, conv_ch, extra=[h_out]),
    )(x.reshape(b * seq, d), meta, g.reshape(1, d), w_in, w_kt)


def _logit_terms(z, mask):
    softplus = jnp.maximum(z, 0.0) + jnp.log(1.0 + jnp.exp(-jnp.abs(z)))
    if mask is not None:
        softplus = jnp.where(mask, softplus, 0.0)
        z = jnp.where(mask, z, LOG_ZERO)
    return softplus, z


def _hi_lo(x):
    hi = x.astype(BF16)
    return hi, (x - hi.astype(F32)).astype(BF16)


def _attn_kernel(q_ref, kt_ref, v_ref, csw_ref, o_ref,
                 kst_ref, vst_ref, hilo0, hilo1, zst0, zst1, ws0, ws1, acc_ref, carry_ref):
    n_blocks = q_ref.shape[1] // LANES
    hilo_s, zst_s, ws_s = (hilo0, hilo1), (zst0, zst1), (ws0, ws1)
    lane = lax.broadcasted_iota(jnp.int32, (LANES, LANES), 1)
    row = lax.broadcasted_iota(jnp.int32, (LANES, LANES), 0)
    diag_mask = lane < row
    zero_half = jnp.zeros((HEAD_DIM, LANES), BF16)
    zero_blk = jnp.zeros((LANES, LANES), BF16)

    def rows(i, n=LANES):
        if isinstance(i, int):
            return pl.ds(i * n, n)
        return pl.ds(pl.multiple_of(i * n, n), n)

    for j in range(BAND - 1):
        kst_ref[j] = jnp.zeros(kst_ref.shape[1:], BF16)
        vst_ref[rows(j), :] = zero_blk
    for j in range(n_blocks):
        kt = kt_ref[:, j * LANES:(j + 1) * LANES]
        top = jnp.concatenate([kt[:HEAD_DIM], zero_half], axis=1)
        bottom = jnp.concatenate([zero_half, kt[HEAD_DIM:]], axis=1)
        kst_ref[j + BAND - 1] = jnp.concatenate([top, bottom], axis=0)
    vst_ref[(BAND - 1) * LANES:, :] = v_ref[0]

    def tile_rows(p, h):
        if p == 0:
            return slice(h * FAR_ROWS, (h + 1) * FAR_ROWS)
        start = HEADS_PER_BLOCK * FAR_ROWS + (HEADS_PER_BLOCK * (p - 1) + h) * LANES
        return slice(start, start + LANES)

    for ws in ws_s:
        ws[...] = jnp.zeros(ws.shape, BF16)

    def stage_logits(i, s, first_valid=0):
        q2 = q_ref[0, rows(i), :]
        for p in range(BAND):
            n_rows = FAR_ROWS if p == 0 else LANES
            if p < first_valid:
                for h in range(HEADS_PER_BLOCK):
                    hilo_s[s][tile_rows(p, h), :] = jnp.zeros((n_rows, 2 * LANES), BF16)
                    zst_s[s][tile_rows(p, h), :] = jnp.full((n_rows, LANES), LOG_ZERO, F32)
                continue
            z2 = jnp.dot(q2[:n_rows], kst_ref[i + p], preferred_element_type=F32)
            for h in range(HEADS_PER_BLOCK):
                softplus, z = _logit_terms(z2[:, h * LANES:(h + 1) * LANES],
                                           diag_mask if p == BAND - 1 else None)
                hi, lo = _hi_lo(softplus)
                hilo_s[s][tile_rows(p, h), 0:LANES] = hi
                hilo_s[s][tile_rows(p, h), LANES:] = lo
                zst_s[s][tile_rows(p, h), :] = z

    def stage_weights(s):
        sums = jnp.dot(hilo_s[s][...], csw_ref[...], preferred_element_type=F32)
        left = None
        for h in range(HEADS_PER_BLOCK):
            carry = jnp.zeros((LANES, LANES), F32)
            head = h * LANES
            for p in reversed(range(1, BAND)):
                blk = tile_rows(p, h)
                e = zst_s[s][blk, :] + sums[blk, :LANES] + carry
                ws_s[s][head:head + LANES, p * LANES:(p + 1) * LANES] = jnp.exp(e).astype(BF16)
                carry = carry + sums[blk, LANES:]
            blk = tile_rows(0, h)
            e = zst_s[s][blk, :] + sums[blk, :LANES] + carry[:FAR_ROWS]
            ws_s[s][head:head + FAR_ROWS, 0:LANES] = jnp.exp(e).astype(BF16)
            carry = jnp.concatenate([carry[:FAR_ROWS] + sums[blk, LANES:], carry[FAR_ROWS:]],
                                    axis=0)
            left = carry if left is None else jnp.maximum(left, carry)
        return left

    def unseen_keys(i):
        return jnp.logical_or(i >= BAND, jnp.logical_and(i >= BAND - 1, row >= FAR_ROWS))

    def stage_output(i, s):
        start = i * LANES
        if not isinstance(i, int):
            start = pl.multiple_of(start, LANES)
        v_band = vst_ref[pl.ds(start, BAND * LANES), :]
        both = jnp.dot(ws_s[s][...], v_band, preferred_element_type=F32)
        out = jnp.where(lane < HEAD_DIM, both[:LANES], both[LANES:])
        o_ref[0, rows(i), :] = out.astype(o_ref.dtype)

    def step(i, s, worst):
        stage_logits(i, s)
        w = stage_weights(1 - s)
        stage_output(i - 2, s)
        return jnp.where(unseen_keys(i - 1), jnp.maximum(worst, w), worst)

    worst = jnp.full((LANES, LANES), -jnp.inf, F32)
    n_pro = min(BAND - 1, n_blocks)
    for i in range(n_pro):
        stage_logits(i, i % 2, first_valid=BAND - 1 - i)
        if i >= 1:
            stage_weights((i - 1) % 2)
        if i >= 2:
            stage_output(i - 2, i % 2)
    n_main = n_blocks - n_pro
    if n_main > 0:
        assert n_pro % 2 == 0

        def pair(t, worst):
            i = n_pro + 2 * t
            worst = step(i, 0, worst)
            return step(i + 1, 1, worst)

        worst = lax.fori_loop(0, n_main // 2, pair, worst)
        if n_main % 2:
            worst = step(n_blocks - 1, (n_blocks - 1) % 2, worst)
    last = n_blocks - 1
    w = stage_weights(last % 2)
    worst = jnp.where(unseen_keys(last), jnp.maximum(worst, w), worst)
    if last >= 1:
        stage_output(last - 1, (last - 1) % 2)
    stage_output(last, last % 2)

    @pl.when(jnp.max(worst) > EXP_ZERO_CUT)
    def _():
        def q_block(i, _):
            q2 = q_ref[0, rows(i), :]
            q_pos = i * LANES + row
            acc_ref[...] = jnp.zeros_like(acc_ref)
            carry_ref[...] = jnp.zeros_like(carry_ref)

            def cond(state):
                j, alive = state
                return jnp.logical_and(j >= 0, alive > 0)

            def body(state):
                j, _ = state
                mask = (j * LANES + lane) < q_pos
                z2 = jnp.dot(q2, kst_ref[j + BAND - 1], preferred_element_type=F32)
                ws = []
                alive = None
                for h in range(HEADS_PER_BLOCK):
                    softplus, z = _logit_terms(z2[:, h * LANES:(h + 1) * LANES], mask)
                    sums = jnp.dot(jnp.concatenate(_hi_lo(softplus), axis=1), csw_ref[...],
                                   preferred_element_type=F32)
                    carry = carry_ref[h]
                    ws.append(jnp.exp(z + sums[:, :LANES] + carry).astype(BF16))
                    carry = carry + sums[:, LANES:]
                    carry_ref[h] = carry
                    alive = carry if alive is None else jnp.maximum(alive, carry)
                acc_ref[...] += jnp.dot(jnp.concatenate(ws, axis=0), v_ref[0, rows(j), :],
                                        preferred_element_type=F32)
                return j - 1, (jnp.max(alive) > EXP_ZERO_CUT).astype(jnp.int32)

            lax.while_loop(cond, body, (i, jnp.int32(1)))
            out = jnp.where(lane < HEAD_DIM, acc_ref[0:LANES, :], acc_ref[LANES:, :])
            o_ref[0, rows(i), :] = out.astype(o_ref.dtype)
            return 0

        lax.fori_loop(0, n_blocks, q_block, 0)


def _cumsum_weights():
    j = jnp.arange(LANES)[:, None]
    s = jnp.arange(LANES)[None, :]
    half = -jnp.concatenate([(j >= s).astype(BF16), jnp.ones((LANES, LANES), BF16)], axis=1)
    return jnp.concatenate([half, half], axis=0)


def _attention(q, kt, v, lp):
    n_pairs, rows, _ = q.shape
    b = rows // lp
    n_blocks = lp // LANES
    spec = pl.BlockSpec((1, lp, LANES), lambda bi, hi: (hi, bi, 0))
    stage = lambda shape, dt: [pltpu.VMEM(shape, dt), pltpu.VMEM(shape, dt)]
    return pl.pallas_call(
        _attn_kernel,
        out_shape=jax.ShapeDtypeStruct(q.shape, BF16),
        grid=(b, n_pairs),
        in_specs=[spec, pl.BlockSpec((LANES, lp), lambda bi, hi: (hi, bi)), spec,
                  _const_spec((2 * LANES, 2 * LANES))],
        out_specs=spec,
        scratch_shapes=[
            pltpu.VMEM((n_blocks + BAND - 1, LANES, 2 * LANES), BF16),
            pltpu.VMEM(((n_blocks + BAND - 1) * LANES, LANES), BF16),
            *stage((STAGED_ROWS, 2 * LANES), BF16),
            *stage((STAGED_ROWS, LANES), F32),
            *stage((HEADS_PER_BLOCK * LANES, BAND * LANES), BF16),
            pltpu.VMEM((HEADS_PER_BLOCK * LANES, LANES), F32),
            pltpu.VMEM((HEADS_PER_BLOCK, LANES, LANES), F32),
        ],
        compiler_params=pltpu.CompilerParams(
            dimension_semantics=("parallel", "parallel"), vmem_limit_bytes=VMEM_LIMIT),
        name="sb_attention",
    )(q, kt, v, _cumsum_weights())


CONV_HALO = 32
CONV_CHUNK = 64
CONV_TILE_ROWS = 1056


def _conv_kernel(u_ref, tail_ref, w_ref, b_ref, o_ref, *, n_taps):
    tl = u_ref.shape[1]
    w = w_ref[...]
    bias = b_ref[...]
    history = jnp.where(pl.program_id(1) > 0, tail_ref[0], 0.0)

    def convolve(window):
        acc = jnp.broadcast_to(bias, (CONV_CHUNK, bias.shape[1]))
        for r in range(8):
            shifted = window if r == 0 else pltpu.roll(window, r, 0)
            for a8 in range(0, n_taps, 8):
                s = a8 + r
                if s >= n_taps:
                    continue
                tap = n_taps - 1 - s
                lo = CONV_HALO - a8
                acc = acc + w[tap:tap + 1, :] * shifted[lo:lo + CONV_CHUNK, :]
        return acc

    o_ref[0, 0:CONV_CHUNK, :] = convolve(
        jnp.concatenate([history, u_ref[0, 0:CONV_CHUNK, :]], axis=0))

    def chunk(c, _):
        base = pl.multiple_of(c * CONV_CHUNK, 8)
        window = u_ref[0, pl.ds(base - CONV_HALO, CONV_CHUNK + CONV_HALO), :]
        o_ref[0, pl.ds(base, CONV_CHUNK), :] = convolve(window)
        return 0

    lax.fori_loop(1, tl // CONV_CHUNK, chunk, 0)


def _conformer_conv(u, dw_w, dw_b):
    b, lp, c = u.shape
    n_taps = dw_w.shape[0]
    assert n_taps - 1 <= CONV_HALO and lp % CONV_CHUNK == 0
    tl = CONV_CHUNK * max(n for n in range(1, lp // CONV_CHUNK + 1)
                          if (lp // CONV_CHUNK) % n == 0 and n * CONV_CHUNK <= CONV_TILE_ROWS)
    per_tile = tl // CONV_HALO
    tile = pl.BlockSpec((1, tl, c), lambda bi, ti: (bi, ti, 0))
    tail = pl.BlockSpec((1, CONV_HALO, c),
                        lambda bi, ti: (bi, jnp.maximum(ti * per_tile - 1, 0), 0))
    return pl.pallas_call(
        functools.partial(_conv_kernel, n_taps=n_taps),
        out_shape=jax.ShapeDtypeStruct((b, lp, c), F32),
        grid=(b, lp // tl),
        in_specs=[tile, tail, _const_spec((n_taps, c)), _const_spec((1, c))],
        out_specs=tile,
        compiler_params=pltpu.CompilerParams(
            dimension_semantics=("parallel", "parallel"), vmem_limit_bytes=VMEM_LIMIT),
        name="conformer_conv",
    )(u, u, dw_w, dw_b.reshape(1, c))


FFN_TILE_ROWS = 512


def _mix_ffn_kernel(h_ref, attn_ref, conv_ref, lng_ref, lnb_ref, wo_ref, g_ref, wg_ref, wu_ref,
                    wd_ref, fg_ref, o_ref, *, final_norm):
    attn = jnp.concatenate([attn_ref[p] for p in range(attn_ref.shape[0])], axis=1)
    sb = attn.shape[-1]
    x = conv_ref[...]
    cen = x - jnp.mean(x, axis=-1, keepdims=True)
    var = jnp.mean(cen * cen, axis=-1, keepdims=True)
    y = cen * lax.rsqrt(var + EPS) * lng_ref[...] + lnb_ref[...]
    conv = (y * jax.nn.sigmoid(y)).astype(BF16)
    h = h_ref[...]
    h = h + jnp.dot(attn, wo_ref[0:sb, :], preferred_element_type=F32)
    h = h + jnp.dot(conv, wo_ref[sb:, :], preferred_element_type=F32)
    hn = _rmsnorm(h, g_ref[...]).astype(BF16)
    gate = jnp.dot(hn, wg_ref[...], preferred_element_type=F32)
    up = jnp.dot(hn, wu_ref[...], preferred_element_type=F32)
    act = (gate * jax.nn.sigmoid(gate) * up).astype(BF16)
    h = h + jnp.dot(act, wd_ref[...], preferred_element_type=F32)
    if final_norm:
        h = _rmsnorm(h, fg_ref[...])
    o_ref[...] = h


def _mix_ffn(h2d, attn, conv2d, ln_g, ln_b, wo, g, wg, wu, wd, final_g, layer, final_norm,
             keep=None):
    rows, d = h2d.shape
    c = conv2d.shape[1]
    n_pairs = attn.shape[0]
    vec = lambda v: v.reshape(1, -1)
    if keep is None:
        tm = _row_tile(rows, FFN_TILE_ROWS)
        out_rows = rows
        in_spec = lambda w: pl.BlockSpec((tm, w), lambda i: (i, 0))
        attn_spec = pl.BlockSpec((n_pairs, tm, LANES), lambda i: (0, i, 0))
    else:
        lp, first, count = keep
        tm = _row_tile(count, FFN_TILE_ROWS)
        per_seq = count // tm
        out_rows = rows // lp * count
        start = lambda i: pl.multiple_of((i // per_seq) * lp + first + (i % per_seq) * tm, 16)
        in_spec = lambda w: pl.BlockSpec((pl.Element(tm), pl.Element(w)),
                                         lambda i: (start(i), 0))
        attn_spec = pl.BlockSpec((pl.Element(n_pairs), pl.Element(tm), pl.Element(LANES)),
                                 lambda i: (0, start(i), 0))
    row_spec = lambda w: pl.BlockSpec((tm, w), lambda i: (i, 0))
    return pl.pallas_call(
        functools.partial(_mix_ffn_kernel, final_norm=final_norm),
        out_shape=jax.ShapeDtypeStruct((out_rows, d), F32),
        grid=(out_rows // tm,),
        in_specs=[
            in_spec(d), attn_spec, in_spec(c),
            _const_spec((1, c)), _const_spec((1, c)),
            _layer_spec(wo, layer), _const_spec((1, d)),
            _layer_spec(wg, layer), _layer_spec(wu, layer), _layer_spec(wd, layer),
            _const_spec((1, d)),
        ],
        out_specs=row_spec(d),
        compiler_params=pltpu.CompilerParams(
            dimension_semantics=("parallel",), vmem_limit_bytes=VMEM_LIMIT),
        name="mix_ffn",
    )(h2d, attn, conv2d, vec(ln_g), vec(ln_b), wo, vec(g), wg, wu, wd, vec(final_g))


def kernel(x, meta_tokens, mix_norm_g, w_in, conv_dw_w, conv_dw_b, conv_ln_g, conv_ln_b,
           w_out, ffn_norm_g, w_gate, w_up, w_down, final_norm_g):
    b, seq, d = x.shape
    n_meta = meta_tokens.shape[0]
    depth = w_in.shape[0]
    conv_ch = conv_dw_w.shape[-1]
    l = n_meta + seq
    lp = -(-l // LANES) * LANES

    first_tm = _first_tile(b, seq, n_meta, lp)
    if first_tm is None:
        meta = jnp.broadcast_to(meta_tokens.astype(x.dtype)[None], (b, n_meta, d))
        h = jnp.concatenate([meta, x, jnp.zeros((b, lp - l, d), x.dtype)], axis=1)
        h = h.reshape(b * lp, d)

    w_in_b, w_out_b, w_gate_b, w_up_b, w_down_b = (
        w.astype(BF16) for w in (w_in, w_out, w_gate, w_up, w_down))
    w_k = lax.optimization_barrier(w_in[:, :, SB_WIDTH:2 * SB_WIDTH])
    w_kt = jnp.swapaxes(w_k, 1, 2).astype(BF16)

    for i in range(depth):
        if i == 0 and first_tm is not None:
            h, q, kt, v, u = _first_in_proj(x, meta_tokens.astype(x.dtype), lp, first_tm,
                                            mix_norm_g[0], w_in_b, w_kt, conv_ch)
        else:
            q, kt, v, u = _in_proj(h, mix_norm_g[i], w_in_b, w_kt, i, conv_ch)
        attn = _attention(q, kt, v, lp)
        conv = _conformer_conv(u.reshape(b, lp, -1), conv_dw_w[i], conv_dw_b[i])
        last = i == depth - 1
        keep = (lp, n_meta, seq) if last and n_meta % 16 == 0 and seq % 16 == 0 else None
        h = _mix_ffn(h, attn, conv.reshape(b * lp, -1),
                     conv_ln_g[i], conv_ln_b[i], w_out_b, ffn_norm_g[i], w_gate_b, w_up_b,
                     w_down_b, final_norm_g, i, final_norm=last, keep=keep)
    if keep is not None:
        return h.reshape(b, seq, d)
    return h.reshape(b, lp, d)[:, n_meta:l]
```

```python
import functools
import math

import jax
import jax.numpy as jnp
from jax import lax
from jax.experimental import pallas as pl
from jax.experimental.pallas import tpu as pltpu

F32 = jnp.float32
BF16 = jnp.bfloat16

EPS = 1e-6
N_HEADS = 8
HEAD_DIM = 64
SB_WIDTH = N_HEADS * HEAD_DIM
LANES = 128
HEADS_PER_BLOCK = LANES // HEAD_DIM
EXP_ZERO_CUT = -104.0
LOG_ZERO = -1e30
BAND = 3
FAR_ROWS = 64
STAGED_ROWS = HEADS_PER_BLOCK * (FAR_ROWS + (BAND - 1) * LANES)
VMEM_LIMIT = 56 * 1024 * 1024


def _row_tile(rows, target):
    best = 8
    for t in range(8, min(rows, target) + 1, 8):
        if rows % t == 0:
            best = t
    return best


def _const_spec(shape):
    zeros = (0,) * len(shape)
    return pl.BlockSpec(shape, lambda *_: zeros, pipeline_mode=pl.Buffered(1))


def _layer_spec(stacked, layer):
    zeros = (0,) * (stacked.ndim - 1)
    return pl.BlockSpec((None,) + stacked.shape[1:], lambda *_: (layer,) + zeros,
                        pipeline_mode=pl.Buffered(1))


def _rmsnorm(x, g):
    ms = jnp.mean(x * x, axis=-1, keepdims=True)
    return x * lax.rsqrt(ms + EPS) * g


def _in_proj_kernel(h_ref, g_ref, w_ref, wkt_ref, q_ref, kt_ref, v_ref, u_ref, *, q_scale):
    _project(h_ref[...], g_ref, w_ref, wkt_ref, q_ref, kt_ref, v_ref, u_ref, q_scale)


def _first_in_proj_kernel(x_ref, meta_ref, g_ref, w_ref, wkt_ref, h_ref, q_ref, kt_ref, v_ref,
                          u_ref, *, q_scale, tiles_per_seq, pad):
    i = pl.program_id(0)
    n_meta = meta_ref.shape[0]
    xt = x_ref[...]
    tm = xt.shape[0]
    late = jnp.concatenate([xt[:n_meta], xt[:tm - n_meta]], axis=0)
    early = jnp.concatenate([xt[pad:], xt[:pad]], axis=0)
    body = jnp.where(i == 0, late, jnp.where(i == pl.num_programs(0) - 1, early, xt))
    top = jnp.where(i % tiles_per_seq == 0, meta_ref[...], body[:n_meta])
    h = jnp.concatenate([top, body[n_meta:]], axis=0)
    h_ref[...] = h
    _project(h, g_ref, w_ref, wkt_ref, q_ref, kt_ref, v_ref, u_ref, q_scale)


def _project(h, g_ref, w_ref, wkt_ref, q_ref, kt_ref, v_ref, u_ref, q_scale):
    hn = _rmsnorm(h, g_ref[...]).astype(BF16)
    sb = q_ref.shape[0] * q_ref.shape[2]
    c = u_ref.shape[-1]
    proj = lambda lo, width: jnp.dot(hn, w_ref[:, lo:lo + width], preferred_element_type=F32)

    def put_pairs(ref, y):
        for p in range(ref.shape[0]):
            ref[p] = y[:, p * LANES:(p + 1) * LANES].astype(ref.dtype)

    put_pairs(q_ref, proj(0, sb) * q_scale)
    put_pairs(v_ref, proj(2 * sb, sb))
    u_ref[...] = proj(3 * sb, c) * jax.nn.sigmoid(proj(3 * sb + c, c))
    kt = lax.dot_general(wkt_ref[...], hn, (((1,), (1,)), ((), ())), preferred_element_type=F32)
    kt_ref[...] = kt.astype(kt_ref.dtype)


def _in_proj(h2d, g, w_in, w_kt, layer, conv_ch):
    rows, d = h2d.shape
    tm = _row_tile(rows, 1024)
    return pl.pallas_call(
        functools.partial(_in_proj_kernel, q_scale=1.0 / math.sqrt(HEAD_DIM)),
        grid=(rows // tm,),
        in_specs=[
            pl.BlockSpec((tm, d), lambda i: (i, 0)),
            _const_spec((1, d)),
            _layer_spec(w_in, layer),
            _layer_spec(w_kt, layer),
        ],
        compiler_params=pltpu.CompilerParams(
            dimension_semantics=("parallel",), vmem_limit_bytes=VMEM_LIMIT),
        name="in_proj",
        **_proj_outputs(rows, tm, conv_ch),
    )(h2d, g.reshape(1, d), w_in, w_kt)


def _proj_outputs(rows, tm, conv_ch, extra=()):
    n_pairs = SB_WIDTH // LANES
    by_pair = (jax.ShapeDtypeStruct((n_pairs, rows, LANES), BF16),
               pl.BlockSpec((n_pairs, tm, LANES), lambda i: (0, i, 0)))
    outs = list(extra) + [
        by_pair,
        (jax.ShapeDtypeStruct((SB_WIDTH, rows), BF16),
         pl.BlockSpec((SB_WIDTH, tm), lambda i: (0, i))),
        by_pair,
        (jax.ShapeDtypeStruct((rows, conv_ch), F32),
         pl.BlockSpec((tm, conv_ch), lambda i: (i, 0)))]
    return dict(out_shape=[o[0] for o in outs], out_specs=[o[1] for o in outs])


FIRST_TILE_ROWS = 1408


def _first_tile(b, seq, n_meta, lp):
    tiles = [t for t in range(LANES, min(lp, FIRST_TILE_ROWS) + 1, LANES) if lp % t == 0]
    tm = max(tiles) if tiles else 0
    ok = (tm and lp // tm >= 2 and b * seq >= tm and n_meta % 8 == 0
          and (lp - n_meta - seq) % 8 == 0)
    return tm if ok else None


def _first_in_proj(x, meta, lp, tm, g, w_in, w_kt, conv_ch):
    b, seq, d = x.shape
    n_meta = meta.shape[0]
    rows = b * lp
    per_seq = lp // tm
    pad = lp - n_meta - seq

    def start(i):
        wanted = (i // per_seq) * seq + (i % per_seq) * tm - n_meta
        return pl.multiple_of(jnp.clip(wanted, 0, b * seq - tm), 8)

    h_out = (jax.ShapeDtypeStruct((rows, d), F32), pl.BlockSpec((tm, d), lambda i: (i, 0)))
    return pl.pallas_call(
        functools.partial(_first_in_proj_kernel, q_scale=1.0 / math.sqrt(HEAD_DIM),
                          tiles_per_seq=per_seq, pad=pad),
        grid=(rows // tm,),
        in_specs=[
            pl.BlockSpec((pl.Element(tm), pl.Element(d)), lambda i: (start(i), 0)),
            _const_spec(meta.shape),
            _const_spec((1, d)),
            _layer_spec(w_in, 0),
            _layer_spec(w_kt, 0),
        ],
        compiler_params=pltpu.CompilerParams(
            dimension_semantics=("parallel",), vmem_limit_bytes=VMEM_LIMIT),
        name="first_in_proj",
        **_proj_outputs(rows, tm, conv_ch, extra=[h_out]),
    )(x.reshape(b * seq, d), meta, g.reshape(1, d), w_in, w_kt)


def _logit_terms(z, mask):
    softplus = jnp.maximum(z, 0.0) + jnp.log(1.0 + jnp.exp(-jnp.abs(z)))
    if mask is not None:
        softplus = jnp.where(mask, softplus, 0.0)
        z = jnp.where(mask, z, LOG_ZERO)
    return softplus, z


def _hi_lo(x):
    hi = x.astype(BF16)
    return hi, (x - hi.astype(F32)).astype(BF16)


def _attn_kernel(q_ref, kt_ref, v_ref, csw_ref, o_ref,
                 kst_ref, vst_ref, hilo0, hilo1, zst0, zst1, ws0, ws1, acc_ref, carry_ref):
    n_blocks = q_ref.shape[1] // LANES
    hilo_s, zst_s, ws_s = (hilo0, hilo1), (zst0, zst1), (ws0, ws1)
    lane = lax.broadcasted_iota(jnp.int32, (LANES, LANES), 1)
    row = lax.broadcasted_iota(jnp.int32, (LANES, LANES), 0)
    diag_mask = lane < row
    zero_half = jnp.zeros((HEAD_DIM, LANES), BF16)
    zero_blk = jnp.zeros((LANES, LANES), BF16)

    def rows(i, n=LANES):
        if isinstance(i, int):
            return pl.ds(i * n, n)
        return pl.ds(pl.multiple_of(i * n, n), n)

    for j in range(BAND - 1):
        kst_ref[j] = jnp.zeros(kst_ref.shape[1:], BF16)
        vst_ref[rows(j), :] = zero_blk
    for j in range(n_blocks):
        kt = kt_ref[:, j * LANES:(j + 1) * LANES]
        top = jnp.concatenate([kt[:HEAD_DIM], zero_half], axis=1)
        bottom = jnp.concatenate([zero_half, kt[HEAD_DIM:]], axis=1)
        kst_ref[j + BAND - 1] = jnp.concatenate([top, bottom], axis=0)
    vst_ref[(BAND - 1) * LANES:, :] = v_ref[0]

    def tile_rows(p, h):
        if p == 0:
            return slice(h * FAR_ROWS, (h + 1) * FAR_ROWS)
        start = HEADS_PER_BLOCK * FAR_ROWS + (HEADS_PER_BLOCK * (p - 1) + h) * LANES
        return slice(start, start + LANES)

    for ws in ws_s:
        ws[...] = jnp.zeros(ws.shape, BF16)

    def stage_logits(i, s, first_valid=0):
        q2 = q_ref[0, rows(i), :]
        for p in range(BAND):
            n_rows = FAR_ROWS if p == 0 else LANES
            if p < first_valid:
                for h in range(HEADS_PER_BLOCK):
                    hilo_s[s][tile_rows(p, h), :] = jnp.zeros((n_rows, 2 * LANES), BF16)
                    zst_s[s][tile_rows(p, h), :] = jnp.full((n_rows, LANES), LOG_ZERO, F32)
                continue
            z2 = jnp.dot(q2[:n_rows], kst_ref[i + p], preferred_element_type=F32)
            for h in range(HEADS_PER_BLOCK):
                softplus, z = _logit_terms(z2[:, h * LANES:(h + 1) * LANES],
                                           diag_mask if p == BAND - 1 else None)
                hi, lo = _hi_lo(softplus)
                hilo_s[s][tile_rows(p, h), 0:LANES] = hi
                hilo_s[s][tile_rows(p, h), LANES:] = lo
                zst_s[s][tile_rows(p, h), :] = z

    def stage_weights(s):
        sums = jnp.dot(hilo_s[s][...], csw_ref[...], preferred_element_type=F32)
        left = None
        for h in range(HEADS_PER_BLOCK):
            carry = jnp.zeros((LANES, LANES), F32)
            head = h * LANES
            for p in reversed(range(1, BAND)):
                blk = tile_rows(p, h)
                e = zst_s[s][blk, :] + sums[blk, :LANES] + carry
                ws_s[s][head:head + LANES, p * LANES:(p + 1) * LANES] = jnp.exp(e).astype(BF16)
                carry = carry + sums[blk, LANES:]
            blk = tile_rows(0, h)
            e = zst_s[s][blk, :] + sums[blk, :LANES] + carry[:FAR_ROWS]
            ws_s[s][head:head + FAR_ROWS, 0:LANES] = jnp.exp(e).astype(BF16)
            carry = jnp.concatenate([carry[:FAR_ROWS] + sums[blk, LANES:], carry[FAR_ROWS:]],
                                    axis=0)
            left = carry if left is None else jnp.maximum(left, carry)
        return left

    def unseen_keys(i):
        return jnp.logical_or(i >= BAND, jnp.logical_and(i >= BAND - 1, row >= FAR_ROWS))

    def stage_output(i, s):
        start = i * LANES
        if not isinstance(i, int):
            start = pl.multiple_of(start, LANES)
        v_band = vst_ref[pl.ds(start, BAND * LANES), :]
        both = jnp.dot(ws_s[s][...], v_band, preferred_element_type=F32)
        out = jnp.where(lane < HEAD_DIM, both[:LANES], both[LANES:])
        o_ref[0, rows(i), :] = out.astype(o_ref.dtype)

    def step(i, s, worst):
        stage_logits(i, s)
        w = stage_weights(1 - s)
        stage_output(i - 2, s)
        return jnp.where(unseen_keys(i - 1), jnp.maximum(worst, w), worst)

    worst = jnp.full((LANES, LANES), -jnp.inf, F32)
    n_pro = min(BAND - 1, n_blocks)
    for i in range(n_pro):
        stage_logits(i, i % 2, first_valid=BAND - 1 - i)
        if i >= 1:
            stage_weights((i - 1) % 2)
        if i >= 2:
            stage_output(i - 2, i % 2)
    n_main = n_blocks - n_pro
    if n_main > 0:
        assert n_pro % 2 == 0

        def pair(t, worst):
            i = n_pro + 2 * t
            worst = step(i, 0, worst)
            return step(i + 1, 1, worst)

        worst = lax.fori_loop(0, n_main // 2, pair, worst)
        if n_main % 2:
            worst = step(n_blocks - 1, (n_blocks - 1) % 2, worst)
    last = n_blocks - 1
    w = stage_weights(last % 2)
    worst = jnp.where(unseen_keys(last), jnp.maximum(worst, w), worst)
    if last >= 1:
        stage_output(last - 1, (last - 1) % 2)
    stage_output(last, last % 2)

    @pl.when(jnp.max(worst) > EXP_ZERO_CUT)
    def _():
        def q_block(i, _):
            q2 = q_ref[0, rows(i), :]
            q_pos = i * LANES + row
            acc_ref[...] = jnp.zeros_like(acc_ref)
            carry_ref[...] = jnp.zeros_like(carry_ref)

            def cond(state):
                j, alive = state
                return jnp.logical_and(j >= 0, alive > 0)

            def body(state):
                j, _ = state
                mask = (j * LANES + lane) < q_pos
                z2 = jnp.dot(q2, kst_ref[j + BAND - 1], preferred_element_type=F32)
                ws = []
                alive = None
                for h in range(HEADS_PER_BLOCK):
                    softplus, z = _logit_terms(z2[:, h * LANES:(h + 1) * LANES], mask)
                    sums = jnp.dot(jnp.concatenate(_hi_lo(softplus), axis=1), csw_ref[...],
                                   preferred_element_type=F32)
                    carry = carry_ref[h]
                    ws.append(jnp.exp(z + sums[:, :LANES] + carry).astype(BF16))
                    carry = carry + sums[:, LANES:]
                    carry_ref[h] = carry
                    alive = carry if alive is None else jnp.maximum(alive, carry)
                acc_ref[...] += jnp.dot(jnp.concatenate(ws, axis=0), v_ref[0, rows(j), :],
                                        preferred_element_type=F32)
                return j - 1, (jnp.max(alive) > EXP_ZERO_CUT).astype(jnp.int32)

            lax.while_loop(cond, body, (i, jnp.int32(1)))
            out = jnp.where(lane < HEAD_DIM, acc_ref[0:LANES, :], acc_ref[LANES:, :])
            o_ref[0, rows(i), :] = out.astype(o_ref.dtype)
            return 0

        lax.fori_loop(0, n_blocks, q_block, 0)


def _cumsum_weights():
    j = jnp.arange(LANES)[:, None]
    s = jnp.arange(LANES)[None, :]
    half = -jnp.concatenate([(j >= s).astype(BF16), jnp.ones((LANES, LANES), BF16)], axis=1)
    return jnp.concatenate([half, half], axis=0)


def _attention(q, kt, v, lp):
    n_pairs, rows, _ = q.shape
    b = rows // lp
    n_blocks = lp // LANES
    spec = pl.BlockSpec((1, lp, LANES), lambda bi, hi: (hi, bi, 0))
    stage = lambda shape, dt: [pltpu.VMEM(shape, dt), pltpu.VMEM(shape, dt)]
    return pl.pallas_call(
        _attn_kernel,
        out_shape=jax.ShapeDtypeStruct(q.shape, BF16),
        grid=(b, n_pairs),
        in_specs=[spec, pl.BlockSpec((LANES, lp), lambda bi, hi: (hi, bi)), spec,
                  _const_spec((2 * LANES, 2 * LANES))],
        out_specs=spec,
        scratch_shapes=[
            pltpu.VMEM((n_blocks + BAND - 1, LANES, 2 * LANES), BF16),
            pltpu.VMEM(((n_blocks + BAND - 1) * LANES, LANES), BF16),
            *stage((STAGED_ROWS, 2 * LANES), BF16),
            *stage((STAGED_ROWS, LANES), F32),
            *stage((HEADS_PER_BLOCK * LANES, BAND * LANES), BF16),
            pltpu.VMEM((HEADS_PER_BLOCK * LANES, LANES), F32),
            pltpu.VMEM((HEADS_PER_BLOCK, LANES, LANES), F32),
        ],
        compiler_params=pltpu.CompilerParams(
            dimension_semantics=("parallel", "parallel"), vmem_limit_bytes=VMEM_LIMIT),
        name="sb_attention",
    )(q, kt, v, _cumsum_weights())


CONV_HALO = 32
CONV_CHUNK = 128
CONV_TILE_ROWS = 1408


def _conv_kernel(u_ref, tail_ref, w_ref, b_ref, o_ref, *, n_taps):
    tl = u_ref.shape[1]
    w = w_ref[...]
    bias = b_ref[...]
    history = jnp.where(pl.program_id(1) > 0, tail_ref[0], 0.0)

    def convolve(window):
        acc = jnp.broadcast_to(bias, (CONV_CHUNK, bias.shape[1]))
        for r in range(8):
            shifted = window if r == 0 else pltpu.roll(window, r, 0)
            for a8 in range(0, n_taps, 8):
                s = a8 + r
                if s >= n_taps:
                    continue
                tap = n_taps - 1 - s
                lo = CONV_HALO - a8
                acc = acc + w[tap:tap + 1, :] * shifted[lo:lo + CONV_CHUNK, :]
        return acc

    o_ref[0, 0:CONV_CHUNK, :] = convolve(
        jnp.concatenate([history, u_ref[0, 0:CONV_CHUNK, :]], axis=0))

    def chunk(c, _):
        base = pl.multiple_of(c * CONV_CHUNK, 8)
        window = u_ref[0, pl.ds(base - CONV_HALO, CONV_CHUNK + CONV_HALO), :]
        o_ref[0, pl.ds(base, CONV_CHUNK), :] = convolve(window)
        return 0

    lax.fori_loop(1, tl // CONV_CHUNK, chunk, 0)


def _conformer_conv(u, dw_w, dw_b):
    b, lp, c = u.shape
    n_taps = dw_w.shape[0]
    assert n_taps - 1 <= CONV_HALO and lp % CONV_CHUNK == 0
    tl = CONV_CHUNK * max(n for n in range(1, lp // CONV_CHUNK + 1)
                          if (lp // CONV_CHUNK) % n == 0 and n * CONV_CHUNK <= CONV_TILE_ROWS)
    per_tile = tl // CONV_HALO
    tile = pl.BlockSpec((1, tl, c), lambda bi, ti: (bi, ti, 0))
    tail = pl.BlockSpec((1, CONV_HALO, c),
                        lambda bi, ti: (bi, jnp.maximum(ti * per_tile - 1, 0), 0))
    return pl.pallas_call(
        functools.partial(_conv_kernel, n_taps=n_taps),
        out_shape=jax.ShapeDtypeStruct((b, lp, c), F32),
        grid=(b, lp // tl),
        in_specs=[tile, tail, _const_spec((n_taps, c)), _const_spec((1, c))],
        out_specs=tile,
        compiler_params=pltpu.CompilerParams(
            dimension_semantics=("parallel", "parallel"), vmem_limit_bytes=VMEM_LIMIT),
        name="conformer_conv",
    )(u, u, dw_w, dw_b.reshape(1, c))


FFN_TILE_ROWS = 512


def _mix_ffn_kernel(h_ref, attn_ref, conv_ref, lng_ref, lnb_ref, wo_ref, g_ref, wg_ref, wu_ref,
                    wd_ref, fg_ref, o_ref, *, final_norm):
    attn = jnp.concatenate([attn_ref[p] for p in range(attn_ref.shape[0])], axis=1)
    sb = attn.shape[-1]
    x = conv_ref[...]
    cen = x - jnp.mean(x, axis=-1, keepdims=True)
    var = jnp.mean(cen * cen, axis=-1, keepdims=True)
    y = cen * lax.rsqrt(var + EPS) * lng_ref[...] + lnb_ref[...]
    conv = (y * jax.nn.sigmoid(y)).astype(BF16)
    h = h_ref[...]
    h = h + jnp.dot(attn, wo_ref[0:sb, :], preferred_element_type=F32)
    h = h + jnp.dot(conv, wo_ref[sb:, :], preferred_element_type=F32)
    hn = _rmsnorm(h, g_ref[...]).astype(BF16)
    gate = jnp.dot(hn, wg_ref[...], preferred_element_type=F32)
    up = jnp.dot(hn, wu_ref[...], preferred_element_type=F32)
    act = (gate * jax.nn.sigmoid(gate) * up).astype(BF16)
    h = h + jnp.dot(act, wd_ref[...], preferred_element_type=F32)
    if final_norm:
        h = _rmsnorm(h, fg_ref[...])
    o_ref[...] = h


def _mix_ffn(h2d, attn, conv2d, ln_g, ln_b, wo, g, wg, wu, wd, final_g, layer, final_norm,
             keep=None):
    rows, d = h2d.shape
    c = conv2d.shape[1]
    n_pairs = attn.shape[0]
    vec = lambda v: v.reshape(1, -1)
    if keep is None:
        tm = _row_tile(rows, FFN_TILE_ROWS)
        out_rows = rows
        in_spec = lambda w: pl.BlockSpec((tm, w), lambda i: (i, 0))
        attn_spec = pl.BlockSpec((n_pairs, tm, LANES), lambda i: (0, i, 0))
    else:
        lp, first, count = keep
        tm = _row_tile(count, FFN_TILE_ROWS)
        per_seq = count // tm
        out_rows = rows // lp * count
        start = lambda i: pl.multiple_of((i // per_seq) * lp + first + (i % per_seq) * tm, 16)
        in_spec = lambda w: pl.BlockSpec((pl.Element(tm), pl.Element(w)),
                                         lambda i: (start(i), 0))
        attn_spec = pl.BlockSpec((pl.Element(n_pairs), pl.Element(tm), pl.Element(LANES)),
                                 lambda i: (0, start(i), 0))
    row_spec = lambda w: pl.BlockSpec((tm, w), lambda i: (i, 0))
    return pl.pallas_call(
        functools.partial(_mix_ffn_kernel, final_norm=final_norm),
        out_shape=jax.ShapeDtypeStruct((out_rows, d), F32),
        grid=(out_rows // tm,),
        in_specs=[
            in_spec(d), attn_spec, in_spec(c),
            _const_spec((1, c)), _const_spec((1, c)),
            _layer_spec(wo, layer), _const_spec((1, d)),
            _layer_spec(wg, layer), _layer_spec(wu, layer), _layer_spec(wd, layer),
            _const_spec((1, d)),
        ],
        out_specs=row_spec(d),
        compiler_params=pltpu.CompilerParams(
            dimension_semantics=("parallel",), vmem_limit_bytes=VMEM_LIMIT),
        name="mix_ffn",
    )(h2d, attn, conv2d, vec(ln_g), vec(ln_b), wo, vec(g), wg, wu, wd, vec(final_g))


def kernel(x, meta_tokens, mix_norm_g, w_in, conv_dw_w, conv_dw_b, conv_ln_g, conv_ln_b,
           w_out, ffn_norm_g, w_gate, w_up, w_down, final_norm_g):
    b, seq, d = x.shape
    n_meta = meta_tokens.shape[0]
    depth = w_in.shape[0]
    conv_ch = conv_dw_w.shape[-1]
    l = n_meta + seq
    lp = -(-l // LANES) * LANES

    first_tm = _first_tile(b, seq, n_meta, lp)
    if first_tm is None:
        meta = jnp.broadcast_to(meta_tokens.astype(x.dtype)[None], (b, n_meta, d))
        h = jnp.concatenate([meta, x, jnp.zeros((b, lp - l, d), x.dtype)], axis=1)
        h = h.reshape(b * lp, d)

    w_in_b, w_out_b, w_gate_b, w_up_b, w_down_b = (
        w.astype(BF16) for w in (w_in, w_out, w_gate, w_up, w_down))
    w_k = lax.optimization_barrier(w_in[:, :, SB_WIDTH:2 * SB_WIDTH])
    w_kt = jnp.swapaxes(w_k, 1, 2).astype(BF16)

    for i in range(depth):
        if i == 0 and first_tm is not None:
            h, q, kt, v, u = _first_in_proj(x, meta_tokens.astype(x.dtype), lp, first_tm,
                                            mix_norm_g[0], w_in_b, w_kt, conv_ch)
        else:
            q, kt, v, u = _in_proj(h, mix_norm_g[i], w_in_b, w_kt, i, conv_ch)
        attn = _attention(q, kt, v, lp)
        conv = _conformer_conv(u.reshape(b, lp, -1), conv_dw_w[i], conv_dw_b[i])
        last = i == depth - 1
        keep = (lp, n_meta, seq) if last and n_meta % 16 == 0 and seq % 16 == 0 else None
        h = _mix_ffn(h, attn, conv.reshape(b * lp, -1),
                     conv_ln_g[i], conv_ln_b[i], w_out_b, ffn_norm_g[i], w_gate_b, w_up_b,
                     w_down_b, final_norm_g, i, final_norm=last, keep=keep)
    if keep is not None:
        return h.reshape(b, seq, d)
    return h.reshape(b, lp, d)[:, n_meta:l]
```

```python
import functools
import math

import jax
import jax.numpy as jnp
from jax import lax
from jax.experimental import pallas as pl
from jax.experimental.pallas import tpu as pltpu

F32 = jnp.float32
BF16 = jnp.bfloat16

EPS = 1e-6
N_HEADS = 8
HEAD_DIM = 64
SB_WIDTH = N_HEADS * HEAD_DIM
LANES = 128
SUBLANES = 8
BF16_ROWS = 2 * SUBLANES
HEADS_PER_BLOCK = LANES // HEAD_DIM
EXP_ZERO_CUT = -104.0
LOG_ZERO = -1e30
BAND = 3
FAR_ROWS = 64
STAGED_ROWS = HEADS_PER_BLOCK * (FAR_ROWS + (BAND - 1) * LANES)
VMEM_LIMIT = 56 * 1024 * 1024


def _row_tile(rows, target):
    best = SUBLANES
    for t in range(SUBLANES, min(rows, target) + 1, SUBLANES):
        if rows % t == 0:
            best = t
    return best


def _const_spec(shape):
    zeros = (0,) * len(shape)
    return pl.BlockSpec(shape, lambda *_: zeros, pipeline_mode=pl.Buffered(1))


def _layer_spec(stacked, layer):
    zeros = (0,) * (stacked.ndim - 1)
    return pl.BlockSpec((None,) + stacked.shape[1:], lambda *_: (layer,) + zeros,
                        pipeline_mode=pl.Buffered(1))


def _rmsnorm(x, g):
    ms = jnp.mean(x * x, axis=-1, keepdims=True)
    return x * lax.rsqrt(ms + EPS) * g


PROJ_TILE_ROWS = 1024


def _in_proj_kernel(h_ref, g_ref, w_ref, wkt_ref, q_ref, kt_ref, v_ref, u_ref, *, q_scale):
    _project(h_ref[...], g_ref, w_ref, wkt_ref, q_ref, kt_ref, v_ref, u_ref, q_scale)


def _first_in_proj_kernel(x_ref, meta_ref, g_ref, w_ref, wkt_ref, h_ref, q_ref, kt_ref, v_ref,
                          u_ref, *, q_scale, tiles_per_seq, pad):
    i = pl.program_id(0)
    n_meta = meta_ref.shape[0]
    xt = x_ref[...]
    tm = xt.shape[0]
    late = jnp.concatenate([xt[:n_meta], xt[:tm - n_meta]], axis=0)
    early = jnp.concatenate([xt[pad:], xt[:pad]], axis=0)
    body = jnp.where(i == 0, late, jnp.where(i == pl.num_programs(0) - 1, early, xt))
    top = jnp.where(i % tiles_per_seq == 0, meta_ref[...], body[:n_meta])
    h = jnp.concatenate([top, body[n_meta:]], axis=0)
    h_ref[...] = h
    _project(h, g_ref, w_ref, wkt_ref, q_ref, kt_ref, v_ref, u_ref, q_scale)


def _project(h, g_ref, w_ref, wkt_ref, q_ref, kt_ref, v_ref, u_ref, q_scale):
    hn = _rmsnorm(h, g_ref[...]).astype(BF16)
    sb = q_ref.shape[0] * q_ref.shape[2]
    c = u_ref.shape[-1]
    proj = lambda lo, width: jnp.dot(hn, w_ref[:, lo:lo + width], preferred_element_type=F32)

    def put_pairs(ref, y):
        for p in range(ref.shape[0]):
            ref[p] = y[:, p * LANES:(p + 1) * LANES].astype(ref.dtype)

    put_pairs(q_ref, proj(0, sb) * q_scale)
    put_pairs(v_ref, proj(2 * sb, sb))
    u_ref[...] = proj(3 * sb, c) * jax.nn.sigmoid(proj(3 * sb + c, c))
    kt = lax.dot_general(wkt_ref[...], hn, (((1,), (1,)), ((), ())), preferred_element_type=F32)
    kt_ref[...] = kt.astype(kt_ref.dtype)


def _in_proj(h2d, g, w_in, w_kt, layer, conv_ch):
    rows, d = h2d.shape
    tm = _row_tile(rows, PROJ_TILE_ROWS)
    return pl.pallas_call(
        functools.partial(_in_proj_kernel, q_scale=1.0 / math.sqrt(HEAD_DIM)),
        grid=(rows // tm,),
        in_specs=[
            pl.BlockSpec((tm, d), lambda i: (i, 0)),
            _const_spec((1, d)),
            _layer_spec(w_in, layer),
            _layer_spec(w_kt, layer),
        ],
        compiler_params=pltpu.CompilerParams(
            dimension_semantics=("parallel",), vmem_limit_bytes=VMEM_LIMIT),
        name="in_proj",
        **_proj_outputs(rows, tm, conv_ch),
    )(h2d, g.reshape(1, d), w_in, w_kt)


def _proj_outputs(rows, tm, conv_ch, extra=()):
    n_pairs = SB_WIDTH // LANES
    by_pair = (jax.ShapeDtypeStruct((n_pairs, rows, LANES), BF16),
               pl.BlockSpec((n_pairs, tm, LANES), lambda i: (0, i, 0)))
    outs = list(extra) + [
        by_pair,
        (jax.ShapeDtypeStruct((SB_WIDTH, rows), BF16),
         pl.BlockSpec((SB_WIDTH, tm), lambda i: (0, i))),
        by_pair,
        (jax.ShapeDtypeStruct((rows, conv_ch), F32),
         pl.BlockSpec((tm, conv_ch), lambda i: (i, 0)))]
    return dict(out_shape=[o[0] for o in outs], out_specs=[o[1] for o in outs])


FIRST_TILE_ROWS = 1408


def _first_tile(b, seq, n_meta, lp):
    tiles = [t for t in range(LANES, min(lp, FIRST_TILE_ROWS) + 1, LANES) if lp % t == 0]
    tm = max(tiles) if tiles else 0
    ok = (tm and lp // tm >= 2 and b * seq >= tm and n_meta % SUBLANES == 0
          and (lp - n_meta - seq) % SUBLANES == 0)
    return tm if ok else None


def _first_in_proj(x, meta, lp, tm, g, w_in, w_kt, conv_ch):
    b, seq, d = x.shape
    n_meta = meta.shape[0]
    rows = b * lp
    per_seq = lp // tm
    pad = lp - n_meta - seq

    def start(i):
        wanted = (i // per_seq) * seq + (i % per_seq) * tm - n_meta
        return pl.multiple_of(jnp.clip(wanted, 0, b * seq - tm), SUBLANES)

    h_out = (jax.ShapeDtypeStruct((rows, d), F32), pl.BlockSpec((tm, d), lambda i: (i, 0)))
    return pl.pallas_call(
        functools.partial(_first_in_proj_kernel, q_scale=1.0 / math.sqrt(HEAD_DIM),
                          tiles_per_seq=per_seq, pad=pad),
        grid=(rows // tm,),
        in_specs=[
            pl.BlockSpec((pl.Element(tm), pl.Element(d)), lambda i: (start(i), 0)),
            _const_spec(meta.shape),
            _const_spec((1, d)),
            _layer_spec(w_in, 0),
            _layer_spec(w_kt, 0),
        ],
        compiler_params=pltpu.CompilerParams(
            dimension_semantics=("parallel",), vmem_limit_bytes=VMEM_LIMIT),
        name="first_in_proj",
        **_proj_outputs(rows, tm, conv_ch, extra=[h_out]),
    )(x.reshape(b * seq, d), meta, g.reshape(1, d), w_in, w_kt)


def _logit_terms(z, mask):
    softplus = jnp.maximum(z, 0.0) + jnp.log(1.0 + jnp.exp(-jnp.abs(z)))
    if mask is not None:
        softplus = jnp.where(mask, softplus, 0.0)
        z = jnp.where(mask, z, LOG_ZERO)
    return softplus, z


def _hi_lo(x):
    hi = x.astype(BF16)
    return hi, (x - hi.astype(F32)).astype(BF16)


def _attn_kernel(q_ref, kt_ref, v_ref, csw_ref, o_ref,
                 kst_ref, vst_ref, hilo0, hilo1, zst0, zst1, ws0, ws1, acc_ref, carry_ref):
    n_blocks = q_ref.shape[1] // LANES
    hilo_s, zst_s, ws_s = (hilo0, hilo1), (zst0, zst1), (ws0, ws1)
    lane = lax.broadcasted_iota(jnp.int32, (LANES, LANES), 1)
    row = lax.broadcasted_iota(jnp.int32, (LANES, LANES), 0)
    diag_mask = lane < row
    zero_half = jnp.zeros((HEAD_DIM, LANES), BF16)
    zero_blk = jnp.zeros((LANES, LANES), BF16)

    def rows(i, n=LANES):
        if isinstance(i, int):
            return pl.ds(i * n, n)
        return pl.ds(pl.multiple_of(i * n, n), n)

    for j in range(BAND - 1):
        kst_ref[j] = jnp.zeros(kst_ref.shape[1:], BF16)
        vst_ref[rows(j), :] = zero_blk
    for j in range(n_blocks):
        kt = kt_ref[:, j * LANES:(j + 1) * LANES]
        top = jnp.concatenate([kt[:HEAD_DIM], zero_half], axis=1)
        bottom = jnp.concatenate([zero_half, kt[HEAD_DIM:]], axis=1)
        kst_ref[j + BAND - 1] = jnp.concatenate([top, bottom], axis=0)
    vst_ref[(BAND - 1) * LANES:, :] = v_ref[0]

    def tile_rows(p, h):
        if p == 0:
            return slice(h * FAR_ROWS, (h + 1) * FAR_ROWS)
        start = HEADS_PER_BLOCK * FAR_ROWS + (HEADS_PER_BLOCK * (p - 1) + h) * LANES
        return slice(start, start + LANES)

    for ws in ws_s:
        ws[...] = jnp.zeros(ws.shape, BF16)

    def stage_logits(i, s, first_valid=0):
        q2 = q_ref[0, rows(i), :]
        for p in range(BAND):
            n_rows = FAR_ROWS if p == 0 else LANES
            if p < first_valid:
                for h in range(HEADS_PER_BLOCK):
                    hilo_s[s][tile_rows(p, h), :] = jnp.zeros((n_rows, 2 * LANES), BF16)
                    zst_s[s][tile_rows(p, h), :] = jnp.full((n_rows, LANES), LOG_ZERO, F32)
                continue
            z2 = jnp.dot(q2[:n_rows], kst_ref[i + p], preferred_element_type=F32)
            for h in range(HEADS_PER_BLOCK):
                softplus, z = _logit_terms(z2[:, h * LANES:(h + 1) * LANES],
                                           diag_mask if p == BAND - 1 else None)
                hi, lo = _hi_lo(softplus)
                hilo_s[s][tile_rows(p, h), 0:LANES] = hi
                hilo_s[s][tile_rows(p, h), LANES:] = lo
                zst_s[s][tile_rows(p, h), :] = z

    def stage_weights(s):
        sums = jnp.dot(hilo_s[s][...], csw_ref[...], preferred_element_type=F32)
        left = None
        for h in range(HEADS_PER_BLOCK):
            carry = jnp.zeros((LANES, LANES), F32)
            head = h * LANES
            for p in reversed(range(1, BAND)):
                blk = tile_rows(p, h)
                e = zst_s[s][blk, :] + sums[blk, :LANES] + carry
                ws_s[s][head:head + LANES, p * LANES:(p + 1) * LANES] = jnp.exp(e).astype(BF16)
                carry = carry + sums[blk, LANES:]
            blk = tile_rows(0, h)
            e = zst_s[s][blk, :] + sums[blk, :LANES] + carry[:FAR_ROWS]
            ws_s[s][head:head + FAR_ROWS, 0:LANES] = jnp.exp(e).astype(BF16)
            carry = jnp.concatenate([carry[:FAR_ROWS] + sums[blk, LANES:], carry[FAR_ROWS:]],
                                    axis=0)
            left = carry if left is None else jnp.maximum(left, carry)
        return left

    def unseen_keys(i):
        return jnp.logical_or(i >= BAND, jnp.logical_and(i >= BAND - 1, row >= FAR_ROWS))

    def stage_output(i, s):
        start = i * LANES
        if not isinstance(i, int):
            start = pl.multiple_of(start, LANES)
        v_band = vst_ref[pl.ds(start, BAND * LANES), :]
        both = jnp.dot(ws_s[s][...], v_band, preferred_element_type=F32)
        out = jnp.where(lane < HEAD_DIM, both[:LANES], both[LANES:])
        o_ref[0, rows(i), :] = out.astype(o_ref.dtype)

    def step(i, s, worst):
        stage_logits(i, s)
        w = stage_weights(1 - s)
        stage_output(i - 2, s)
        return jnp.where(unseen_keys(i - 1), jnp.maximum(worst, w), worst)

    worst = jnp.full((LANES, LANES), -jnp.inf, F32)
    n_pro = min(BAND - 1, n_blocks)
    for i in range(n_pro):
        stage_logits(i, i % 2, first_valid=BAND - 1 - i)
        if i >= 1:
            stage_weights((i - 1) % 2)
        if i >= 2:
            stage_output(i - 2, i % 2)
    n_main = n_blocks - n_pro
    if n_main > 0:
        assert n_pro % 2 == 0

        def pair(t, worst):
            i = n_pro + 2 * t
            worst = step(i, 0, worst)
            return step(i + 1, 1, worst)

        worst = lax.fori_loop(0, n_main // 2, pair, worst)
        if n_main % 2:
            worst = step(n_blocks - 1, (n_blocks - 1) % 2, worst)
    last = n_blocks - 1
    w = stage_weights(last % 2)
    worst = jnp.where(unseen_keys(last), jnp.maximum(worst, w), worst)
    if last >= 1:
        stage_output(last - 1, (last - 1) % 2)
    stage_output(last, last % 2)

    @pl.when(jnp.max(worst) > EXP_ZERO_CUT)
    def _():
        def q_block(i, _):
            q2 = q_ref[0, rows(i), :]
            q_pos = i * LANES + row
            acc_ref[...] = jnp.zeros_like(acc_ref)
            carry_ref[...] = jnp.zeros_like(carry_ref)

            def cond(state):
                j, alive = state
                return jnp.logical_and(j >= 0, alive > 0)

            def body(state):
                j, _ = state
                mask = (j * LANES + lane) < q_pos
                z2 = jnp.dot(q2, kst_ref[j + BAND - 1], preferred_element_type=F32)
                ws = []
                alive = None
                for h in range(HEADS_PER_BLOCK):
                    softplus, z = _logit_terms(z2[:, h * LANES:(h + 1) * LANES], mask)
                    sums = jnp.dot(jnp.concatenate(_hi_lo(softplus), axis=1), csw_ref[...],
                                   preferred_element_type=F32)
                    carry = carry_ref[h]
                    ws.append(jnp.exp(z + sums[:, :LANES] + carry).astype(BF16))
                    carry = carry + sums[:, LANES:]
                    carry_ref[h] = carry
                    alive = carry if alive is None else jnp.maximum(alive, carry)
                acc_ref[...] += jnp.dot(jnp.concatenate(ws, axis=0), v_ref[0, rows(j), :],
                                        preferred_element_type=F32)
                return j - 1, (jnp.max(alive) > EXP_ZERO_CUT).astype(jnp.int32)

            lax.while_loop(cond, body, (i, jnp.int32(1)))
            out = jnp.where(lane < HEAD_DIM, acc_ref[0:LANES, :], acc_ref[LANES:, :])
            o_ref[0, rows(i), :] = out.astype(o_ref.dtype)
            return 0

        lax.fori_loop(0, n_blocks, q_block, 0)


def _cumsum_weights():
    j = jnp.arange(LANES)[:, None]
    s = jnp.arange(LANES)[None, :]
    half = -jnp.concatenate([(j >= s).astype(BF16), jnp.ones((LANES, LANES), BF16)], axis=1)
    return jnp.concatenate([half, half], axis=0)


def _attention(q, kt, v, lp):
    n_pairs, rows, _ = q.shape
    b = rows // lp
    n_blocks = lp // LANES
    spec = pl.BlockSpec((1, lp, LANES), lambda bi, hi: (hi, bi, 0))
    stage = lambda shape, dt: [pltpu.VMEM(shape, dt), pltpu.VMEM(shape, dt)]
    return pl.pallas_call(
        _attn_kernel,
        out_shape=jax.ShapeDtypeStruct(q.shape, BF16),
        grid=(b, n_pairs),
        in_specs=[spec, pl.BlockSpec((LANES, lp), lambda bi, hi: (hi, bi)), spec,
                  _const_spec((2 * LANES, 2 * LANES))],
        out_specs=spec,
        scratch_shapes=[
            pltpu.VMEM((n_blocks + BAND - 1, LANES, 2 * LANES), BF16),
            pltpu.VMEM(((n_blocks + BAND - 1) * LANES, LANES), BF16),
            *stage((STAGED_ROWS, 2 * LANES), BF16),
            *stage((STAGED_ROWS, LANES), F32),
            *stage((HEADS_PER_BLOCK * LANES, BAND * LANES), BF16),
            pltpu.VMEM((HEADS_PER_BLOCK * LANES, LANES), F32),
            pltpu.VMEM((HEADS_PER_BLOCK, LANES, LANES), F32),
        ],
        compiler_params=pltpu.CompilerParams(
            dimension_semantics=("parallel", "parallel"), vmem_limit_bytes=VMEM_LIMIT),
        name="sb_attention",
    )(q, kt, v, _cumsum_weights())


CONV_HALO = 32
CONV_CHUNK = 128
CONV_TILE_ROWS = 1408


def _conv_kernel(u_ref, tail_ref, w_ref, b_ref, o_ref, *, n_taps):
    tl = u_ref.shape[1]
    w = w_ref[...]
    bias = b_ref[...]
    history = jnp.where(pl.program_id(1) > 0, tail_ref[0], 0.0)

    def convolve(window):
        acc = jnp.broadcast_to(bias, (CONV_CHUNK, bias.shape[1]))
        for r in range(SUBLANES):
            shifted = window if r == 0 else pltpu.roll(window, r, 0)
            for a8 in range(0, n_taps, SUBLANES):
                s = a8 + r
                if s >= n_taps:
                    continue
                tap = n_taps - 1 - s
                lo = CONV_HALO - a8
                acc = acc + w[tap:tap + 1, :] * shifted[lo:lo + CONV_CHUNK, :]
        return acc

    o_ref[0, 0:CONV_CHUNK, :] = convolve(
        jnp.concatenate([history, u_ref[0, 0:CONV_CHUNK, :]], axis=0))

    def chunk(c, _):
        base = pl.multiple_of(c * CONV_CHUNK, SUBLANES)
        window = u_ref[0, pl.ds(base - CONV_HALO, CONV_CHUNK + CONV_HALO), :]
        o_ref[0, pl.ds(base, CONV_CHUNK), :] = convolve(window)
        return 0

    lax.fori_loop(1, tl // CONV_CHUNK, chunk, 0)


def _conformer_conv(u, dw_w, dw_b):
    b, lp, c = u.shape
    n_taps = dw_w.shape[0]
    assert n_taps - 1 <= CONV_HALO and lp % CONV_CHUNK == 0
    tl = CONV_CHUNK * max(n for n in range(1, lp // CONV_CHUNK + 1)
                          if (lp // CONV_CHUNK) % n == 0 and n * CONV_CHUNK <= CONV_TILE_ROWS)
    per_tile = tl // CONV_HALO
    tile = pl.BlockSpec((1, tl, c), lambda bi, ti: (bi, ti, 0))
    tail = pl.BlockSpec((1, CONV_HALO, c),
                        lambda bi, ti: (bi, jnp.maximum(ti * per_tile - 1, 0), 0))
    return pl.pallas_call(
        functools.partial(_conv_kernel, n_taps=n_taps),
        out_shape=jax.ShapeDtypeStruct((b, lp, c), F32),
        grid=(b, lp // tl),
        in_specs=[tile, tail, _const_spec((n_taps, c)), _const_spec((1, c))],
        out_specs=tile,
        compiler_params=pltpu.CompilerParams(
            dimension_semantics=("parallel", "parallel"), vmem_limit_bytes=VMEM_LIMIT),
        name="conformer_conv",
    )(u, u, dw_w, dw_b.reshape(1, c))


FFN_TILE_ROWS = 512


def _mix_ffn_kernel(h_ref, attn_ref, conv_ref, lng_ref, lnb_ref, wo_ref, g_ref, wg_ref, wu_ref,
                    wd_ref, fg_ref, o_ref, *, final_norm):
    attn = jnp.concatenate([attn_ref[p] for p in range(attn_ref.shape[0])], axis=1)
    sb = attn.shape[-1]
    x = conv_ref[...]
    cen = x - jnp.mean(x, axis=-1, keepdims=True)
    var = jnp.mean(cen * cen, axis=-1, keepdims=True)
    y = cen * lax.rsqrt(var + EPS) * lng_ref[...] + lnb_ref[...]
    conv = (y * jax.nn.sigmoid(y)).astype(BF16)
    h = h_ref[...]
    h = h + jnp.dot(attn, wo_ref[0:sb, :], preferred_element_type=F32)
    h = h + jnp.dot(conv, wo_ref[sb:, :], preferred_element_type=F32)
    hn = _rmsnorm(h, g_ref[...]).astype(BF16)
    gate = jnp.dot(hn, wg_ref[...], preferred_element_type=F32)
    up = jnp.dot(hn, wu_ref[...], preferred_element_type=F32)
    act = (gate * jax.nn.sigmoid(gate) * up).astype(BF16)
    h = h + jnp.dot(act, wd_ref[...], preferred_element_type=F32)
    if final_norm:
        h = _rmsnorm(h, fg_ref[...])
    o_ref[...] = h


def _mix_ffn(h2d, attn, conv2d, ln_g, ln_b, wo, g, wg, wu, wd, final_g, layer, final_norm,
             keep=None):
    rows, d = h2d.shape
    c = conv2d.shape[1]
    n_pairs = attn.shape[0]
    vec = lambda v: v.reshape(1, -1)
    if keep is None:
        tm = _row_tile(rows, FFN_TILE_ROWS)
        out_rows = rows
        in_spec = lambda w: pl.BlockSpec((tm, w), lambda i: (i, 0))
        attn_spec = pl.BlockSpec((n_pairs, tm, LANES), lambda i: (0, i, 0))
    else:
        lp, first, count = keep
        tm = _row_tile(count, FFN_TILE_ROWS)
        per_seq = count // tm
        out_rows = rows // lp * count
        start = lambda i: pl.multiple_of((i // per_seq) * lp + first + (i % per_seq) * tm,
                                         BF16_ROWS)
        in_spec = lambda w: pl.BlockSpec((pl.Element(tm), pl.Element(w)),
                                         lambda i: (start(i), 0))
        attn_spec = pl.BlockSpec((pl.Element(n_pairs), pl.Element(tm), pl.Element(LANES)),
                                 lambda i: (0, start(i), 0))
    row_spec = lambda w: pl.BlockSpec((tm, w), lambda i: (i, 0))
    return pl.pallas_call(
        functools.partial(_mix_ffn_kernel, final_norm=final_norm),
        out_shape=jax.ShapeDtypeStruct((out_rows, d), F32),
        grid=(out_rows // tm,),
        in_specs=[
            in_spec(d), attn_spec, in_spec(c),
            _const_spec((1, c)), _const_spec((1, c)),
            _layer_spec(wo, layer), _const_spec((1, d)),
            _layer_spec(wg, layer), _layer_spec(wu, layer), _layer_spec(wd, layer),
            _const_spec((1, d)),
        ],
        out_specs=row_spec(d),
        compiler_params=pltpu.CompilerParams(
            dimension_semantics=("parallel",), vmem_limit_bytes=VMEM_LIMIT),
        name="mix_ffn",
    )(h2d, attn, conv2d, vec(ln_g), vec(ln_b), wo, vec(g), wg, wu, wd, vec(final_g))


def kernel(x, meta_tokens, mix_norm_g, w_in, conv_dw_w, conv_dw_b, conv_ln_g, conv_ln_b,
           w_out, ffn_norm_g, w_gate, w_up, w_down, final_norm_g):
    b, seq, d = x.shape
    n_meta = meta_tokens.shape[0]
    depth = w_in.shape[0]
    conv_ch = conv_dw_w.shape[-1]
    l = n_meta + seq
    lp = -(-l // LANES) * LANES

    first_tm = _first_tile(b, seq, n_meta, lp)
    if first_tm is None:
        meta = jnp.broadcast_to(meta_tokens.astype(x.dtype)[None], (b, n_meta, d))
        h = jnp.concatenate([meta, x, jnp.zeros((b, lp - l, d), x.dtype)], axis=1)
        h = h.reshape(b * lp, d)

    w_in_b, w_out_b, w_gate_b, w_up_b, w_down_b = (
        w.astype(BF16) for w in (w_in, w_out, w_gate, w_up, w_down))
    w_k = lax.optimization_barrier(w_in[:, :, SB_WIDTH:2 * SB_WIDTH])
    w_kt = jnp.swapaxes(w_k, 1, 2).astype(BF16)

    for i in range(depth):
        if i == 0 and first_tm is not None:
            h, q, kt, v, u = _first_in_proj(x, meta_tokens.astype(x.dtype), lp, first_tm,
                                            mix_norm_g[0], w_in_b, w_kt, conv_ch)
        else:
            q, kt, v, u = _in_proj(h, mix_norm_g[i], w_in_b, w_kt, i, conv_ch)
        attn = _attention(q, kt, v, lp)
        conv = _conformer_conv(u.reshape(b, lp, -1), conv_dw_w[i], conv_dw_b[i])
        last = i == depth - 1
        aligned = n_meta % BF16_ROWS == 0 and seq % BF16_ROWS == 0
        keep = (lp, n_meta, seq) if last and aligned else None
        h = _mix_ffn(h, attn, conv.reshape(b * lp, -1),
                     conv_ln_g[i], conv_ln_b[i], w_out_b, ffn_norm_g[i], w_gate_b, w_up_b,
                     w_down_b, final_norm_g, i, final_norm=last, keep=keep)
    if keep is not None:
        return h.reshape(b, seq, d)
    return h.reshape(b, lp, d)[:, n_meta:l]
```

```python
import functools
import math

import jax
import jax.numpy as jnp
from jax import lax
from jax.experimental import pallas as pl
from jax.experimental.pallas import tpu as pltpu

F32 = jnp.float32
BF16 = jnp.bfloat16

EPS = 1e-6
N_HEADS = 8
HEAD_DIM = 64
SB_WIDTH = N_HEADS * HEAD_DIM
LANES = 128
SUBLANES = 8
BF16_ROWS = 2 * SUBLANES
HEADS_PER_BLOCK = LANES // HEAD_DIM
EXP_ZERO_CUT = -104.0
LOG_ZERO = -1e30
BAND = 3
PAIRS_PER_STEP = 4
FAR_ROWS = 64
STAGED_ROWS = HEADS_PER_BLOCK * (FAR_ROWS + (BAND - 1) * LANES)
VMEM_LIMIT = 56 * 1024 * 1024


def _row_tile(rows, target):
    best = SUBLANES
    for t in range(SUBLANES, min(rows, target) + 1, SUBLANES):
        if rows % t == 0:
            best = t
    return best


def _const_spec(shape):
    zeros = (0,) * len(shape)
    return pl.BlockSpec(shape, lambda *_: zeros, pipeline_mode=pl.Buffered(1))


def _layer_spec(stacked, layer):
    zeros = (0,) * (stacked.ndim - 1)
    return pl.BlockSpec((None,) + stacked.shape[1:], lambda *_: (layer,) + zeros,
                        pipeline_mode=pl.Buffered(1))


def _rmsnorm(x, g):
    ms = jnp.mean(x * x, axis=-1, keepdims=True)
    return x * lax.rsqrt(ms + EPS) * g


PROJ_TILE_ROWS = 1024


def _in_proj_kernel(h_ref, g_ref, w_ref, wkt_ref, q_ref, kt_ref, v_ref, u_ref, *, q_scale):
    _project(h_ref[...], g_ref, w_ref, wkt_ref, q_ref, kt_ref, v_ref, u_ref, q_scale)


def _first_in_proj_kernel(x_ref, meta_ref, g_ref, w_ref, wkt_ref, h_ref, q_ref, kt_ref, v_ref,
                          u_ref, *, q_scale, tiles_per_seq, pad):
    i = pl.program_id(0)
    n_meta = meta_ref.shape[0]
    xt = x_ref[...]
    tm = xt.shape[0]
    late = jnp.concatenate([xt[:n_meta], xt[:tm - n_meta]], axis=0)
    early = jnp.concatenate([xt[pad:], xt[:pad]], axis=0)
    body = jnp.where(i == 0, late, jnp.where(i == pl.num_programs(0) - 1, early, xt))
    top = jnp.where(i % tiles_per_seq == 0, meta_ref[...], body[:n_meta])
    h = jnp.concatenate([top, body[n_meta:]], axis=0)
    h_ref[...] = h
    _project(h, g_ref, w_ref, wkt_ref, q_ref, kt_ref, v_ref, u_ref, q_scale)


def _project(h, g_ref, w_ref, wkt_ref, q_ref, kt_ref, v_ref, u_ref, q_scale):
    hn = _rmsnorm(h, g_ref[...]).astype(BF16)
    sb = q_ref.shape[0] * q_ref.shape[2]
    c = u_ref.shape[-1]
    proj = lambda lo, width: jnp.dot(hn, w_ref[:, lo:lo + width], preferred_element_type=F32)

    def put_pairs(ref, y):
        for p in range(ref.shape[0]):
            ref[p] = y[:, p * LANES:(p + 1) * LANES].astype(ref.dtype)

    put_pairs(q_ref, proj(0, sb) * q_scale)
    put_pairs(v_ref, proj(2 * sb, sb))
    u_ref[...] = proj(3 * sb, c) * jax.nn.sigmoid(proj(3 * sb + c, c))
    kt = lax.dot_general(wkt_ref[...], hn, (((1,), (1,)), ((), ())), preferred_element_type=F32)
    kt_ref[...] = kt.astype(kt_ref.dtype)


def _in_proj(h2d, g, w_in, w_kt, layer, conv_ch):
    rows, d = h2d.shape
    tm = _row_tile(rows, PROJ_TILE_ROWS)
    return pl.pallas_call(
        functools.partial(_in_proj_kernel, q_scale=1.0 / math.sqrt(HEAD_DIM)),
        grid=(rows // tm,),
        in_specs=[
            pl.BlockSpec((tm, d), lambda i: (i, 0)),
            _const_spec((1, d)),
            _layer_spec(w_in, layer),
            _layer_spec(w_kt, layer),
        ],
        compiler_params=pltpu.CompilerParams(
            dimension_semantics=("parallel",), vmem_limit_bytes=VMEM_LIMIT),
        name="in_proj",
        **_proj_outputs(rows, tm, conv_ch),
    )(h2d, g.reshape(1, d), w_in, w_kt)


def _proj_outputs(rows, tm, conv_ch, extra=()):
    n_pairs = SB_WIDTH // LANES
    by_pair = (jax.ShapeDtypeStruct((n_pairs, rows, LANES), BF16),
               pl.BlockSpec((n_pairs, tm, LANES), lambda i: (0, i, 0)))
    outs = list(extra) + [
        by_pair,
        (jax.ShapeDtypeStruct((SB_WIDTH, rows), BF16),
         pl.BlockSpec((SB_WIDTH, tm), lambda i: (0, i))),
        by_pair,
        (jax.ShapeDtypeStruct((rows, conv_ch), F32),
         pl.BlockSpec((tm, conv_ch), lambda i: (i, 0)))]
    return dict(out_shape=[o[0] for o in outs], out_specs=[o[1] for o in outs])


FIRST_TILE_ROWS = 1408


def _first_tile(b, seq, n_meta, lp):
    tiles = [t for t in range(LANES, min(lp, FIRST_TILE_ROWS) + 1, LANES) if lp % t == 0]
    tm = max(tiles) if tiles else 0
    ok = (tm and lp // tm >= 2 and b * seq >= tm and n_meta % SUBLANES == 0
          and (lp - n_meta - seq) % SUBLANES == 0)
    return tm if ok else None


def _first_in_proj(x, meta, lp, tm, g, w_in, w_kt, conv_ch):
    b, seq, d = x.shape
    n_meta = meta.shape[0]
    rows = b * lp
    per_seq = lp // tm
    pad = lp - n_meta - seq

    def start(i):
        wanted = (i // per_seq) * seq + (i % per_seq) * tm - n_meta
        return pl.multiple_of(jnp.clip(wanted, 0, b * seq - tm), SUBLANES)

    h_out = (jax.ShapeDtypeStruct((rows, d), F32), pl.BlockSpec((tm, d), lambda i: (i, 0)))
    return pl.pallas_call(
        functools.partial(_first_in_proj_kernel, q_scale=1.0 / math.sqrt(HEAD_DIM),
                          tiles_per_seq=per_seq, pad=pad),
        grid=(rows // tm,),
        in_specs=[
            pl.BlockSpec((pl.Element(tm), pl.Element(d)), lambda i: (start(i), 0)),
            _const_spec(meta.shape),
            _const_spec((1, d)),
            _layer_spec(w_in, 0),
            _layer_spec(w_kt, 0),
        ],
        compiler_params=pltpu.CompilerParams(
            dimension_semantics=("parallel",), vmem_limit_bytes=VMEM_LIMIT),
        name="first_in_proj",
        **_proj_outputs(rows, tm, conv_ch, extra=[h_out]),
    )(x.reshape(b * seq, d), meta, g.reshape(1, d), w_in, w_kt)


def _logit_terms(z, mask):
    softplus = jnp.maximum(z, 0.0) + jnp.log(1.0 + jnp.exp(-jnp.abs(z)))
    if mask is not None:
        softplus = jnp.where(mask, softplus, 0.0)
        z = jnp.where(mask, z, LOG_ZERO)
    return softplus, z


def _hi_lo(x):
    hi = x.astype(BF16)
    return hi, (x - hi.astype(F32)).astype(BF16)


def _attn_kernel(q_ref, kt_ref, v_ref, csw_ref, o_ref,
                 kst_ref, vst_ref, hilo0, hilo1, zst0, zst1, ws0, ws1, acc_ref, carry_ref):
    n_groups = q_ref.shape[0]
    n_blocks = q_ref.shape[1] // LANES
    hilo_s, zst_s, ws_s = (hilo0, hilo1), (zst0, zst1), (ws0, ws1)
    lane = lax.broadcasted_iota(jnp.int32, (LANES, LANES), 1)
    row = lax.broadcasted_iota(jnp.int32, (LANES, LANES), 0)
    diag_mask = lane < row
    zero_half = jnp.zeros((HEAD_DIM, LANES), BF16)
    zero_blk = jnp.zeros((LANES, LANES), BF16)

    def rows(i, n=LANES):
        if isinstance(i, int):
            return pl.ds(i * n, n)
        return pl.ds(pl.multiple_of(i * n, n), n)

    for g in range(n_groups):
        for j in range(BAND - 1):
            kst_ref[g, j] = jnp.zeros(kst_ref.shape[2:], BF16)
            vst_ref[g, rows(j), :] = zero_blk
        for j in range(n_blocks):
            kt = kt_ref[g * LANES:(g + 1) * LANES, j * LANES:(j + 1) * LANES]
            top = jnp.concatenate([kt[:HEAD_DIM], zero_half], axis=1)
            bottom = jnp.concatenate([zero_half, kt[HEAD_DIM:]], axis=1)
            kst_ref[g, j + BAND - 1] = jnp.concatenate([top, bottom], axis=0)
        vst_ref[g, (BAND - 1) * LANES:, :] = v_ref[g]

    def tile_rows(p, h):
        if p == 0:
            return slice(h * FAR_ROWS, (h + 1) * FAR_ROWS)
        start = HEADS_PER_BLOCK * FAR_ROWS + (HEADS_PER_BLOCK * (p - 1) + h) * LANES
        return slice(start, start + LANES)

    for ws in ws_s:
        ws[...] = jnp.zeros(ws.shape, BF16)

    def stage_logits(g, i, s, first_valid=0):
        hilo, zst = hilo_s[s].at[g], zst_s[s].at[g]
        q2 = q_ref[g, rows(i), :]
        for p in range(BAND):
            n_rows = FAR_ROWS if p == 0 else LANES
            if p < first_valid:
                for h in range(HEADS_PER_BLOCK):
                    hilo[tile_rows(p, h), :] = jnp.zeros((n_rows, 2 * LANES), BF16)
                    zst[tile_rows(p, h), :] = jnp.full((n_rows, LANES), LOG_ZERO, F32)
                continue
            z2 = jnp.dot(q2[:n_rows], kst_ref[g, i + p], preferred_element_type=F32)
            for h in range(HEADS_PER_BLOCK):
                softplus, z = _logit_terms(z2[:, h * LANES:(h + 1) * LANES],
                                           diag_mask if p == BAND - 1 else None)
                hi, lo = _hi_lo(softplus)
                hilo[tile_rows(p, h), 0:LANES] = hi
                hilo[tile_rows(p, h), LANES:] = lo
                zst[tile_rows(p, h), :] = z

    def stage_weights(g, s):
        zst, ws = zst_s[s].at[g], ws_s[s].at[g]
        sums = jnp.dot(hilo_s[s][g], csw_ref[...], preferred_element_type=F32)
        left = None
        for h in range(HEADS_PER_BLOCK):
            carry = jnp.zeros((LANES, LANES), F32)
            head = h * LANES
            for p in reversed(range(1, BAND)):
                blk = tile_rows(p, h)
                e = zst[blk, :] + sums[blk, :LANES] + carry
                ws[head:head + LANES, p * LANES:(p + 1) * LANES] = jnp.exp(e).astype(BF16)
                carry = carry + sums[blk, LANES:]
            blk = tile_rows(0, h)
            e = zst[blk, :] + sums[blk, :LANES] + carry[:FAR_ROWS]
            ws[head:head + FAR_ROWS, 0:LANES] = jnp.exp(e).astype(BF16)
            carry = jnp.concatenate([carry[:FAR_ROWS] + sums[blk, LANES:], carry[FAR_ROWS:]],
                                    axis=0)
            left = carry if left is None else jnp.maximum(left, carry)
        return left

    def unseen_keys(i):
        return jnp.logical_or(i >= BAND, jnp.logical_and(i >= BAND - 1, row >= FAR_ROWS))

    def stage_output(g, i, s):
        start = i * LANES
        if not isinstance(i, int):
            start = pl.multiple_of(start, LANES)
        v_band = vst_ref[g, pl.ds(start, BAND * LANES), :]
        both = jnp.dot(ws_s[s][g], v_band, preferred_element_type=F32)
        out = jnp.where(lane < HEAD_DIM, both[:LANES], both[LANES:])
        o_ref[g, rows(i), :] = out.astype(o_ref.dtype)

    def step(i, s, worst):
        for g in range(n_groups):
            stage_logits(g, i, s)
            w = stage_weights(g, 1 - s)
            stage_output(g, i - 2, s)
            worst = jnp.where(unseen_keys(i - 1), jnp.maximum(worst, w), worst)
        return worst

    worst = jnp.full((LANES, LANES), -jnp.inf, F32)
    n_pro = min(BAND - 1, n_blocks)
    for i in range(n_pro):
        for g in range(n_groups):
            stage_logits(g, i, i % 2, first_valid=BAND - 1 - i)
            if i >= 1:
                stage_weights(g, (i - 1) % 2)
            if i >= 2:
                stage_output(g, i - 2, i % 2)
    n_main = n_blocks - n_pro
    if n_main > 0:
        assert n_pro % 2 == 0

        def pair(t, worst):
            i = n_pro + 2 * t
            worst = step(i, 0, worst)
            return step(i + 1, 1, worst)

        worst = lax.fori_loop(0, n_main // 2, pair, worst)
        if n_main % 2:
            worst = step(n_blocks - 1, (n_blocks - 1) % 2, worst)
    last = n_blocks - 1
    for g in range(n_groups):
        w = stage_weights(g, last % 2)
        worst = jnp.where(unseen_keys(last), jnp.maximum(worst, w), worst)
        if last >= 1:
            stage_output(g, last - 1, (last - 1) % 2)
        stage_output(g, last, last % 2)

    def sweep_block(g, i, _):
        q2 = q_ref[g, rows(i), :]
        q_pos = i * LANES + row
        acc_ref[...] = jnp.zeros_like(acc_ref)
        carry_ref[...] = jnp.zeros_like(carry_ref)

        def cond(state):
            j, alive = state
            return jnp.logical_and(j >= 0, alive > 0)

        def body(state):
            j, _ = state
            mask = (j * LANES + lane) < q_pos
            z2 = jnp.dot(q2, kst_ref[g, j + BAND - 1], preferred_element_type=F32)
            ws = []
            alive = None
            for h in range(HEADS_PER_BLOCK):
                softplus, z = _logit_terms(z2[:, h * LANES:(h + 1) * LANES], mask)
                sums = jnp.dot(jnp.concatenate(_hi_lo(softplus), axis=1), csw_ref[...],
                               preferred_element_type=F32)
                carry = carry_ref[h]
                ws.append(jnp.exp(z + sums[:, :LANES] + carry).astype(BF16))
                carry = carry + sums[:, LANES:]
                carry_ref[h] = carry
                alive = carry if alive is None else jnp.maximum(alive, carry)
            acc_ref[...] += jnp.dot(jnp.concatenate(ws, axis=0), v_ref[g, rows(j), :],
                                    preferred_element_type=F32)
            return j - 1, (jnp.max(alive) > EXP_ZERO_CUT).astype(jnp.int32)

        lax.while_loop(cond, body, (i, jnp.int32(1)))
        out = jnp.where(lane < HEAD_DIM, acc_ref[0:LANES, :], acc_ref[LANES:, :])
        o_ref[g, rows(i), :] = out.astype(o_ref.dtype)
        return 0

    @pl.when(jnp.max(worst) > EXP_ZERO_CUT)
    def _():
        for g in range(n_groups):
            lax.fori_loop(0, n_blocks, functools.partial(sweep_block, g), 0)


def _cumsum_weights():
    j = jnp.arange(LANES)[:, None]
    s = jnp.arange(LANES)[None, :]
    half = -jnp.concatenate([(j >= s).astype(BF16), jnp.ones((LANES, LANES), BF16)], axis=1)
    return jnp.concatenate([half, half], axis=0)


def _attention(q, kt, v, lp):
    n_pairs, rows, _ = q.shape
    b = rows // lp
    n_blocks = lp // LANES
    g = PAIRS_PER_STEP if n_pairs % PAIRS_PER_STEP == 0 else 1
    spec = pl.BlockSpec((g, lp, LANES), lambda bi, hi: (hi, bi, 0))
    stage = lambda shape, dt: [pltpu.VMEM((g,) + shape, dt), pltpu.VMEM((g,) + shape, dt)]
    return pl.pallas_call(
        _attn_kernel,
        out_shape=jax.ShapeDtypeStruct(q.shape, BF16),
        grid=(b, n_pairs // g),
        in_specs=[spec, pl.BlockSpec((g * LANES, lp), lambda bi, hi: (hi, bi)), spec,
                  _const_spec((2 * LANES, 2 * LANES))],
        out_specs=spec,
        scratch_shapes=[
            pltpu.VMEM((g, n_blocks + BAND - 1, LANES, 2 * LANES), BF16),
            pltpu.VMEM((g, (n_blocks + BAND - 1) * LANES, LANES), BF16),
            *stage((STAGED_ROWS, 2 * LANES), BF16),
            *stage((STAGED_ROWS, LANES), F32),
            *stage((HEADS_PER_BLOCK * LANES, BAND * LANES), BF16),
            pltpu.VMEM((HEADS_PER_BLOCK * LANES, LANES), F32),
            pltpu.VMEM((HEADS_PER_BLOCK, LANES, LANES), F32),
        ],
        compiler_params=pltpu.CompilerParams(
            dimension_semantics=("parallel", "parallel"), vmem_limit_bytes=VMEM_LIMIT),
        name="sb_attention",
    )(q, kt, v, _cumsum_weights())


CONV_HALO = 32
CONV_CHUNK = 128
CONV_TILE_ROWS = 1408


def _conv_kernel(u_ref, tail_ref, w_ref, b_ref, o_ref, *, n_taps):
    tl = u_ref.shape[1]
    w = w_ref[...]
    bias = b_ref[...]
    history = jnp.where(pl.program_id(1) > 0, tail_ref[0], 0.0)

    def convolve(window):
        acc = jnp.broadcast_to(bias, (CONV_CHUNK, bias.shape[1]))
        for r in range(SUBLANES):
            shifted = window if r == 0 else pltpu.roll(window, r, 0)
            for a8 in range(0, n_taps, SUBLANES):
                s = a8 + r
                if s >= n_taps:
                    continue
                tap = n_taps - 1 - s
                lo = CONV_HALO - a8
                acc = acc + w[tap:tap + 1, :] * shifted[lo:lo + CONV_CHUNK, :]
        return acc

    o_ref[0, 0:CONV_CHUNK, :] = convolve(
        jnp.concatenate([history, u_ref[0, 0:CONV_CHUNK, :]], axis=0))

    def chunk(c, _):
        base = pl.multiple_of(c * CONV_CHUNK, SUBLANES)
        window = u_ref[0, pl.ds(base - CONV_HALO, CONV_CHUNK + CONV_HALO), :]
        o_ref[0, pl.ds(base, CONV_CHUNK), :] = convolve(window)
        return 0

    lax.fori_loop(1, tl // CONV_CHUNK, chunk, 0)


def _conformer_conv(u, dw_w, dw_b):
    b, lp, c = u.shape
    n_taps = dw_w.shape[0]
    assert n_taps - 1 <= CONV_HALO and lp % CONV_CHUNK == 0
    tl = CONV_CHUNK * max(n for n in range(1, lp // CONV_CHUNK + 1)
                          if (lp // CONV_CHUNK) % n == 0 and n * CONV_CHUNK <= CONV_TILE_ROWS)
    per_tile = tl // CONV_HALO
    tile = pl.BlockSpec((1, tl, c), lambda bi, ti: (bi, ti, 0))
    tail = pl.BlockSpec((1, CONV_HALO, c),
                        lambda bi, ti: (bi, jnp.maximum(ti * per_tile - 1, 0), 0))
    return pl.pallas_call(
        functools.partial(_conv_kernel, n_taps=n_taps),
        out_shape=jax.ShapeDtypeStruct((b, lp, c), F32),
        grid=(b, lp // tl),
        in_specs=[tile, tail, _const_spec((n_taps, c)), _const_spec((1, c))],
        out_specs=tile,
        compiler_params=pltpu.CompilerParams(
            dimension_semantics=("parallel", "parallel"), vmem_limit_bytes=VMEM_LIMIT),
        name="conformer_conv",
    )(u, u, dw_w, dw_b.reshape(1, c))


FFN_TILE_ROWS = 512


def _mix_ffn_kernel(h_ref, attn_ref, conv_ref, lng_ref, lnb_ref, wo_ref, g_ref, wg_ref, wu_ref,
                    wd_ref, fg_ref, o_ref, *, final_norm):
    attn = jnp.concatenate([attn_ref[p] for p in range(attn_ref.shape[0])], axis=1)
    sb = attn.shape[-1]
    x = conv_ref[...]
    cen = x - jnp.mean(x, axis=-1, keepdims=True)
    var = jnp.mean(cen * cen, axis=-1, keepdims=True)
    y = cen * lax.rsqrt(var + EPS) * lng_ref[...] + lnb_ref[...]
    conv = (y * jax.nn.sigmoid(y)).astype(BF16)
    h = h_ref[...]
    h = h + jnp.dot(attn, wo_ref[0:sb, :], preferred_element_type=F32)
    h = h + jnp.dot(conv, wo_ref[sb:, :], preferred_element_type=F32)
    hn = _rmsnorm(h, g_ref[...]).astype(BF16)
    gate = jnp.dot(hn, wg_ref[...], preferred_element_type=F32)
    up = jnp.dot(hn, wu_ref[...], preferred_element_type=F32)
    act = (gate * jax.nn.sigmoid(gate) * up).astype(BF16)
    h = h + jnp.dot(act, wd_ref[...], preferred_element_type=F32)
    if final_norm:
        h = _rmsnorm(h, fg_ref[...])
    o_ref[...] = h


def _mix_ffn(h2d, attn, conv2d, ln_g, ln_b, wo, g, wg, wu, wd, final_g, layer, final_norm,
             keep=None):
    rows, d = h2d.shape
    c = conv2d.shape[1]
    n_pairs = attn.shape[0]
    vec = lambda v: v.reshape(1, -1)
    if keep is None:
        tm = _row_tile(rows, FFN_TILE_ROWS)
        out_rows = rows
        in_spec = lambda w: pl.BlockSpec((tm, w), lambda i: (i, 0))
        attn_spec = pl.BlockSpec((n_pairs, tm, LANES), lambda i: (0, i, 0))
    else:
        lp, first, count = keep
        tm = _row_tile(count, FFN_TILE_ROWS)
        per_seq = count // tm
        out_rows = rows // lp * count
        start = lambda i: pl.multiple_of((i // per_seq) * lp + first + (i % per_seq) * tm,
                                         BF16_ROWS)
        in_spec = lambda w: pl.BlockSpec((pl.Element(tm), pl.Element(w)),
                                         lambda i: (start(i), 0))
        attn_spec = pl.BlockSpec((pl.Element(n_pairs), pl.Element(tm), pl.Element(LANES)),
                                 lambda i: (0, start(i), 0))
    row_spec = lambda w: pl.BlockSpec((tm, w), lambda i: (i, 0))
    return pl.pallas_call(
        functools.partial(_mix_ffn_kernel, final_norm=final_norm),
        out_shape=jax.ShapeDtypeStruct((out_rows, d), F32),
        grid=(out_rows // tm,),
        in_specs=[
            in_spec(d), attn_spec, in_spec(c),
            _const_spec((1, c)), _const_spec((1, c)),
            _layer_spec(wo, layer), _const_spec((1, d)),
            _layer_spec(wg, layer), _layer_spec(wu, layer), _layer_spec(wd, layer),
            _const_spec((1, d)),
        ],
        out_specs=row_spec(d),
        compiler_params=pltpu.CompilerParams(
            dimension_semantics=("parallel",), vmem_limit_bytes=VMEM_LIMIT),
        name="mix_ffn",
    )(h2d, attn, conv2d, vec(ln_g), vec(ln_b), wo, vec(g), wg, wu, wd, vec(final_g))


def kernel(x, meta_tokens, mix_norm_g, w_in, conv_dw_w, conv_dw_b, conv_ln_g, conv_ln_b,
           w_out, ffn_norm_g, w_gate, w_up, w_down, final_norm_g):
    b, seq, d = x.shape
    n_meta = meta_tokens.shape[0]
    depth = w_in.shape[0]
    conv_ch = conv_dw_w.shape[-1]
    l = n_meta + seq
    lp = -(-l // LANES) * LANES

    first_tm = _first_tile(b, seq, n_meta, lp)
    if first_tm is None:
        meta = jnp.broadcast_to(meta_tokens.astype(x.dtype)[None], (b, n_meta, d))
        h = jnp.concatenate([meta, x, jnp.zeros((b, lp - l, d), x.dtype)], axis=1)
        h = h.reshape(b * lp, d)

    w_in_b, w_out_b, w_gate_b, w_up_b, w_down_b = (
        w.astype(BF16) for w in (w_in, w_out, w_gate, w_up, w_down))
    w_k = lax.optimization_barrier(w_in[:, :, SB_WIDTH:2 * SB_WIDTH])
    w_kt = jnp.swapaxes(w_k, 1, 2).astype(BF16)

    for i in range(depth):
        if i == 0 and first_tm is not None:
            h, q, kt, v, u = _first_in_proj(x, meta_tokens.astype(x.dtype), lp, first_tm,
                                            mix_norm_g[0], w_in_b, w_kt, conv_ch)
        else:
            q, kt, v, u = _in_proj(h, mix_norm_g[i], w_in_b, w_kt, i, conv_ch)
        attn = _attention(q, kt, v, lp)
        conv = _conformer_conv(u.reshape(b, lp, -1), conv_dw_w[i], conv_dw_b[i])
        last = i == depth - 1
        aligned = n_meta % BF16_ROWS == 0 and seq % BF16_ROWS == 0
        keep = (lp, n_meta, seq) if last and aligned else None
        h = _mix_ffn(h, attn, conv.reshape(b * lp, -1),
                     conv_ln_g[i], conv_ln_b[i], w_out_b, ffn_norm_g[i], w_gate_b, w_up_b,
                     w_down_b, final_norm_g, i, final_norm=last, keep=keep)
    if keep is not None:
        return h.reshape(b, seq, d)
    return h.reshape(b, lp, d)[:, n_meta:l]
```
